```python
import math
import jax
import jax.numpy as jnp
from jax import lax
import numpy as np

D_MODEL = 1024
BATCH = 2
SEQ = 8192
DEPTH = 1

D_MIX = D_MODEL
ATT_WIDTH = D_MIX // 2
GDN_WIDTH = D_MIX - ATT_WIDTH
ATT_HEAD_DIM = 64
ATT_HEADS = ATT_WIDTH // ATT_HEAD_DIM
Q_LORA_RANK = D_MODEL // 4
IDX_HEADS = 4
IDX_DIM = 64
IDX_ROPE_DIM = 32
TOPK_MAX = 256
Q_BLOCK = 128
GDN_HEAD_DIM = 128
GDN_HEADS = GDN_WIDTH // GDN_HEAD_DIM
GDN_CONV = 4
GDN_CHUNK = 64
D_FF = 2816
ROPE_THETA = 10000.0
NORM_EPS = 1e-6
IN_SPLIT_SIZES = (Q_LORA_RANK, ATT_WIDTH, ATT_WIDTH, IDX_DIM, IDX_HEADS,
                  GDN_WIDTH, GDN_WIDTH, GDN_WIDTH, GDN_WIDTH, GDN_HEADS, GDN_HEADS)
D_IN = sum(IN_SPLIT_SIZES)
IN_SPLIT_POINTS = tuple(sum(IN_SPLIT_SIZES[:i + 1]) for i in range(len(IN_SPLIT_SIZES) - 1))

kernel_name = 'hybrid_dsa_gdn_macaron_layer'

F32 = jnp.float32


def rms_norm(x, g):
    xf = x.astype(F32)
    y = xf * lax.rsqrt(jnp.mean(xf * xf, axis=-1, keepdims=True) + NORM_EPS)
    return (y * g.astype(F32)).astype(x.dtype)


def layer_norm(x, g, b):
    xf = x.astype(F32)
    mu = jnp.mean(xf, axis=-1, keepdims=True)
    xc = xf - mu
    var = jnp.mean(xc * xc, axis=-1, keepdims=True)
    return (xc * lax.rsqrt(var + NORM_EPS) * g.astype(F32) + b.astype(F32)).astype(x.dtype)


def l2_normalize(x):
    xf = x.astype(F32)
    return xf * lax.rsqrt(jnp.sum(xf * xf, axis=-1, keepdims=True) + NORM_EPS)


def rotary(x, pos, rot_dim):
    half = rot_dim // 2
    inv_freq = 1.0 / (ROPE_THETA ** (jnp.arange(half, dtype=F32) / half))
    ang = pos.astype(F32)[:, None] * inv_freq[None, :]
    cos = jnp.cos(ang)[None, :, None, :].astype(x.dtype)
    sin = jnp.sin(ang)[None, :, None, :].astype(x.dtype)
    x1 = x[..., :half]
    x2 = x[..., half:rot_dim]
    return jnp.concatenate([x1 * cos - x2 * sin, x2 * cos + x1 * sin, x[..., rot_dim:]], axis=-1)


def swiglu(x, w_gate, w_up, w_down):
    return (jax.nn.silu(x @ w_gate) * (x @ w_up)) @ w_down


def causal_depthwise_conv(x, w):
    width = w.shape[0]
    L = x.shape[1]
    xp = jnp.pad(x, ((0, 0), (width - 1, 0), (0, 0)))
    out = xp[:, 0:L] * w[0]
    for j in range(1, width):
        out = out + xp[:, j:j + L] * w[j]
    return out


def dsa_sparse_attention(q, k, v, q_idx, k_idx, w_idx):
    B, L, H, Dh = q.shape
    top_k = min(TOPK_MAX, L // 4)
    nb = L // Q_BLOCK
    key_pos = jnp.arange(L, dtype=jnp.int32)
    k_idx_f = k_idx.astype(F32)
    gather = jax.vmap(lambda t, i: t[i])

    def blocks(t):
        return jnp.moveaxis(t.reshape((B, nb, Q_BLOCK) + t.shape[2:]), 1, 0)

    def one_block(args):
        qb, qib, wb, qpos = args
        s = jnp.einsum('bqhd,bsd->bqhs', qib.astype(F32), k_idx_f)
        score = jnp.einsum('bqh,bqhs->bqs', wb.astype(F32), jax.nn.relu(s))
        causal = key_pos[None, :] <= qpos[:, None]
        score = jnp.where(causal[None], score, -jnp.inf)
        _, sel = lax.top_k(score, top_k)
        k_sel = gather(k, sel)
        v_sel = gather(v, sel)
        valid = sel <= qpos[None, :, None]
        logits = jnp.einsum('bqhd,bqkhd->bqhk', qb.astype(F32), k_sel.astype(F32)) * (Dh ** -0.5)
        logits = jnp.where(valid[:, :, None, :], logits, -jnp.inf)
        p = jax.nn.softmax(logits, axis=-1).astype(v.dtype)
        return jnp.einsum('bqhk,bqkhd->bqhd', p, v_sel)

    qpos_blocks = key_pos.reshape(nb, Q_BLOCK)
    out = lax.map(one_block, (blocks(q), blocks(q_idx), blocks(w_idx), qpos_blocks))
    return jnp.moveaxis(out, 0, 1).reshape(B, L, H, Dh)


def gated_delta_rule(q, k, v, g, beta):
    B, L, H, Dk = q.shape
    Dv = v.shape[-1]
    C = GDN_CHUNK
    N = L // C

    def to_chunks(t):
        return t.reshape((B, N, C) + t.shape[2:]).swapaxes(2, 3)

    q = to_chunks(q) * (Dk ** -0.5)
    k = to_chunks(k)
    v = to_chunks(v)
    beta = to_chunks(beta)
    g = jnp.cumsum(to_chunks(g), axis=-1)
    incl = jnp.tril(jnp.ones((C, C), dtype=bool))
    strict = jnp.tril(jnp.ones((C, C), dtype=bool), -1)
    decay = jnp.exp(jnp.where(incl, g[..., :, None] - g[..., None, :], -jnp.inf))
    k_beta = k * beta[..., None]
    v_beta = v * beta[..., None]
    a_mat = jnp.where(strict, jnp.einsum('bnhid,bnhjd->bnhij', k_beta, k) * decay, 0.0)
    rhs = jnp.concatenate([v_beta, k_beta * jnp.exp(g)[..., None]], axis=-1)
    sol = lax.linalg.triangular_solve(a_mat, rhs, left_side=True, lower=True, unit_diagonal=True)
    u = sol[..., :Dv]
    w = sol[..., Dv:]
    qk = jnp.where(incl, jnp.einsum('bnhid,bnhjd->bnhij', q, k) * decay, 0.0)
    g_last = g[..., -1]
    q_dec = q * jnp.exp(g)[..., None]
    k_tail = k * jnp.exp(g_last[..., None] - g)[..., None]

    def step(S, inp):
        q_c, qk_c, u_c, w_c, kt_c, gl_c = inp
        v_new = u_c - jnp.einsum('bhcd,bhde->bhce', w_c, S)
        o = jnp.einsum('bhcd,bhde->bhce', q_c, S) + jnp.einsum('bhij,bhje->bhie', qk_c, v_new)
        S = S * jnp.exp(gl_c)[..., None, None] + jnp.einsum('bhcd,bhce->bhde', kt_c, v_new)
        return S, o

    xs = tuple(jnp.moveaxis(t, 1, 0) for t in (q_dec, qk, u, w, k_tail, g_last))
    S0 = jnp.zeros((B, H, Dk, Dv), F32)
    _, o = lax.scan(step, S0, xs)
    return o.transpose(1, 0, 3, 2, 4).reshape(B, L, H, Dv)


def hybrid_mixer(h, w_in, q_a_norm_g, w_q_b, w_qidx_b, kidx_ln_g, kidx_ln_b,
                 gdn_conv_w, gdn_a_log, gdn_dt_bias, gdn_out_norm_g, w_out):
    B, L, _ = h.shape
    dt = h.dtype
    proj = h @ w_in
    c_q, k_att, v_att, k_idx, w_idx, gq, gk, gv, gz, gb, ga = jnp.split(proj, IN_SPLIT_POINTS, axis=-1)
    pos = jnp.arange(L, dtype=jnp.int32)

    c_q = rms_norm(c_q, q_a_norm_g)
    q = (c_q @ w_q_b).reshape(B, L, ATT_HEADS, ATT_HEAD_DIM)
    q_idx = (c_q @ w_qidx_b).reshape(B, L, IDX_HEADS, IDX_DIM)
    k_att = k_att.reshape(B, L, ATT_HEADS, ATT_HEAD_DIM)
    v_att = v_att.reshape(B, L, ATT_HEADS, ATT_HEAD_DIM)
    k_idx = layer_norm(k_idx, kidx_ln_g, kidx_ln_b)
    q = rotary(q, pos, ATT_HEAD_DIM)
    k_att = rotary(k_att, pos, ATT_HEAD_DIM)
    q_idx = rotary(q_idx, pos, IDX_ROPE_DIM)
    k_idx = rotary(k_idx[:, :, None, :], pos, IDX_ROPE_DIM)[:, :, 0, :]
    att = dsa_sparse_attention(q, k_att, v_att, q_idx, k_idx, w_idx).reshape(B, L, ATT_WIDTH)

    qkv = jax.nn.silu(causal_depthwise_conv(jnp.concatenate([gq, gk, gv], axis=-1), gdn_conv_w))
    gq, gk, gv = jnp.split(qkv, (GDN_WIDTH, 2 * GDN_WIDTH), axis=-1)
    gq = l2_normalize(gq.reshape(B, L, GDN_HEADS, GDN_HEAD_DIM))
    gk = l2_normalize(gk.reshape(B, L, GDN_HEADS, GDN_HEAD_DIM))
    gv = gv.reshape(B, L, GDN_HEADS, GDN_HEAD_DIM).astype(F32)
    beta = jax.nn.sigmoid(gb.astype(F32))
    log_decay = -jnp.exp(gdn_a_log.astype(F32)) * jax.nn.softplus(ga.astype(F32) + gdn_dt_bias.astype(F32))
    o = gated_delta_rule(gq, gk, gv, log_decay, beta)
    z = gz.reshape(B, L, GDN_HEADS, GDN_HEAD_DIM).astype(F32)
    o = rms_norm(o, gdn_out_norm_g) * jax.nn.silu(z)
    gdn = o.reshape(B, L, GDN_WIDTH).astype(dt)

    return jnp.concatenate([att, gdn], axis=-1) @ w_out


def setup_inputs(seed: int = 0) -> dict:
    key = jax.random.key(seed)
    ks = jax.random.split(key, 32)

    def nrm(k, shape, fan_in):
        return jax.random.normal(k, shape, F32) * (fan_in ** -0.5)

    def gain(k, shape):
        return 1.0 + 0.02 * jax.random.normal(k, shape, F32)

    Dp = DEPTH
    x = jax.random.normal(ks[0], (BATCH, SEQ, D_MODEL), F32)
    dt_init = jnp.exp(jax.random.uniform(ks[20], (Dp, GDN_HEADS), F32, math.log(1e-3), math.log(1e-1)))
    return {
        'x': x,
        'ffn1_pre_g': gain(ks[1], (Dp, D_MODEL)),
        'ffn1_post_g': gain(ks[2], (Dp, D_MODEL)),
        'ffn1_w_gate': nrm(ks[3], (Dp, D_MODEL, D_FF), D_MODEL),
        'ffn1_w_up': nrm(ks[4], (Dp, D_MODEL, D_FF), D_MODEL),
        'ffn1_w_down': nrm(ks[5], (Dp, D_FF, D_MODEL), D_FF),
        'mix_pre_g': gain(ks[6], (Dp, D_MODEL)),
        'mix_post_g': gain(ks[7], (Dp, D_MODEL)),
        'w_in': nrm(ks[8], (Dp, D_MODEL, D_IN), D_MODEL),
        'q_a_norm_g': gain(ks[9], (Dp, Q_LORA_RANK)),
        'w_q_b': nrm(ks[10], (Dp, Q_LORA_RANK, ATT_WIDTH), Q_LORA_RANK),
        'w_qidx_b': nrm(ks[11], (Dp, Q_LORA_RANK, IDX_HEADS * IDX_DIM), Q_LORA_RANK),
        'kidx_ln_g': gain(ks[12], (Dp, IDX_DIM)),
        'kidx_ln_b': 0.02 * jax.random.normal(ks[13], (Dp, IDX_DIM), F32),
        'gdn_conv_w': nrm(ks[14], (Dp, GDN_CONV, 3 * GDN_WIDTH), GDN_CONV),
        'gdn_a_log': jnp.log(jax.random.uniform(ks[15], (Dp, GDN_HEADS), F32, 1.0, 16.0)),
        'gdn_dt_bias': dt_init + jnp.log(-jnp.expm1(-dt_init)),
        'gdn_out_norm_g': gain(ks[16], (Dp, GDN_HEAD_DIM)),
        'w_out': nrm(ks[17], (Dp, D_MIX, D_MODEL), D_MIX),
        'ffn2_pre_g': gain(ks[18], (Dp, D_MODEL)),
        'ffn2_post_g': gain(ks[19], (Dp, D_MODEL)),
        'ffn2_w_gate': nrm(ks[21], (Dp, D_MODEL, D_FF), D_MODEL),
        'ffn2_w_up': nrm(ks[22], (Dp, D_MODEL, D_FF), D_MODEL),
        'ffn2_w_down': nrm(ks[23], (Dp, D_FF, D_MODEL), D_FF),
    }


def reference(x, ffn1_pre_g, ffn1_post_g, ffn1_w_gate, ffn1_w_up, ffn1_w_down,
              mix_pre_g, mix_post_g, w_in, q_a_norm_g, w_q_b, w_qidx_b, kidx_ln_g, kidx_ln_b,
              gdn_conv_w, gdn_a_log, gdn_dt_bias, gdn_out_norm_g, w_out,
              ffn2_pre_g, ffn2_post_g, ffn2_w_gate, ffn2_w_up, ffn2_w_down):
    for l in range(DEPTH):
        f = swiglu(rms_norm(x, ffn1_pre_g[l]), ffn1_w_gate[l], ffn1_w_up[l], ffn1_w_down[l])
        x = x + 0.5 * rms_norm(f, ffn1_post_g[l])
        m = hybrid_mixer(rms_norm(x, mix_pre_g[l]), w_in[l], q_a_norm_g[l], w_q_b[l], w_qidx_b[l],
                         kidx_ln_g[l], kidx_ln_b[l], gdn_conv_w[l], gdn_a_log[l], gdn_dt_bias[l],
                         gdn_out_norm_g[l], w_out[l])
        x = x + rms_norm(m, mix_post_g[l])
        f = swiglu(rms_norm(x, ffn2_pre_g[l]), ffn2_w_gate[l], ffn2_w_up[l], ffn2_w_down[l])
        x = x + 0.5 * rms_norm(f, ffn2_post_g[l])
    return x
```

```python
import functools
import math

import jax
import jax.numpy as jnp
from jax import lax
from jax.experimental import pallas as pl
from jax.experimental.pallas import tpu as pltpu

F32 = jnp.float32
BF16 = jnp.bfloat16
I32 = jnp.int32

LANES = 128
ATT_HEAD_DIM = 64
ATT_HEADS = 8
IDX_HEADS = 4
IDX_DIM = 64
IDX_ROPE_DIM = 32
Q_LORA_RANK = 256
TOPK_MAX = 256
GDN_HEAD_DIM = 128
GDN_HEADS = 4
GDN_CONV = 4
GDN_CHUNK = 64
ROPE_THETA = 10000.0
NORM_EPS = 1e-6
NEG_BIG = -1e30
INT_MIN = -2147483648
INT_MAX = 2147483647
HIGHEST = lax.Precision.HIGHEST

VMEM_LIMIT = 56 * 1024 * 1024


def _rms(x, g):
    return x * lax.rsqrt(jnp.mean(x * x, axis=-1, keepdims=True) + NORM_EPS) * g


def _dot(a, b):
    return jnp.dot(a, b, preferred_element_type=F32)


def _dot_nt(a, b, precision=None):
    return lax.dot_general(a, b, (((1,), (1,)), ((), ())), precision=precision,
                           preferred_element_type=F32)


def _dot_hi(a, b):
    return jnp.dot(a, b, precision=HIGHEST, preferred_element_type=F32)


def _ffn_body(x_ref, pre_ref, post_ref, wg_ref, wu_ref, wd_ref, o_ref, xn_ref, acc_ref):
    j = pl.program_id(1)

    @pl.when(j == 0)
    def _():
        xn_ref[...] = _rms(x_ref[...], pre_ref[...]).astype(BF16)
        acc_ref[...] = jnp.zeros_like(acc_ref)

    xn = xn_ref[...]
    g = _dot(xn, wg_ref[...])
    u = _dot(xn, wu_ref[...])
    h = (g * jax.nn.sigmoid(g) * u).astype(BF16)
    acc_ref[...] += _dot(h, wd_ref[...])

    @pl.when(j == pl.num_programs(1) - 1)
    def _():
        o_ref[...] = x_ref[...] + 0.5 * _rms(acc_ref[...], post_ref[...])


def _ffn(x, pre_g, post_g, wg, wu, wd, *, tm, tf):
    T, D = x.shape
    F = wg.shape[1]
    return pl.pallas_call(
        _ffn_body,
        grid=(T // tm, F // tf),
        in_specs=[
            pl.BlockSpec((tm, D), lambda i, j: (i, 0)),
            pl.BlockSpec((1, D), lambda i, j: (0, 0)),
            pl.BlockSpec((1, D), lambda i, j: (0, 0)),
            pl.BlockSpec((D, tf), lambda i, j: (0, j)),
            pl.BlockSpec((D, tf), lambda i, j: (0, j)),
            pl.BlockSpec((tf, D), lambda i, j: (j, 0)),
        ],
        out_specs=pl.BlockSpec((tm, D), lambda i, j: (i, 0)),
        out_shape=jax.ShapeDtypeStruct((T, D), F32),
        scratch_shapes=[pltpu.VMEM((tm, D), BF16), pltpu.VMEM((tm, D), F32)],
        compiler_params=pltpu.CompilerParams(
            dimension_semantics=("parallel", "arbitrary"), vmem_limit_bytes=VMEM_LIMIT),
        name="ffn",
    )(x, pre_g, post_g, wg, wu, wd)


def _rope(x, cos, sin_signed, half):
    n, W = x.shape
    lane = lax.broadcasted_iota(I32, (n, LANES), 1)
    first = (lane & 63) < half
    outs = []
    for s in range(W // LANES):
        xs = x[:, s * LANES:(s + 1) * LANES]
        partner = jnp.where(first, pltpu.roll(xs, LANES - half, 1), pltpu.roll(xs, half, 1))
        outs.append(xs * cos + partner * sin_signed)
    return outs[0] if len(outs) == 1 else jnp.concatenate(outs, axis=1)


def _prep_body(x_ref, pre_ref, watt_ref, wsm_ref, wgdn_ref, wz_ref, qan_ref, wqb_ref, wqib_ref,
               lng_ref, lnb_ref, conv_ref, alog_ref, dtb_ref, cosa_ref, sina_ref, cosi_ref, sini_ref,
               q_o, k_o, v_o, qi_o, ki_o, w_o, gq_o, gk_o, gv_o, z_o, beta_o, g_o,
               pbuf, *, tm):
    tile = pl.program_id(1)
    h = _rms(x_ref[0], pre_ref[...]).astype(BF16)
    cosa, sina = cosa_ref[...], sina_ref[...]
    cosi, sini = cosi_ref[...], sini_ref[...]

    pa = _dot(h, watt_ref[...])
    cq = _rms(pa[:, :Q_LORA_RANK], qan_ref[...]).astype(BF16)
    q = _rope(_dot(cq, wqb_ref[...]), cosa, sina, ATT_HEAD_DIM // 2)
    q_o[0] = (q * (ATT_HEAD_DIM ** -0.5)).astype(BF16)
    qi_o[0] = _rope(_dot(cq, wqib_ref[...]), cosi, sini, IDX_ROPE_DIM // 2).astype(BF16)
    k_o[0] = _rope(pa[:, 256:768], cosa, sina, ATT_HEAD_DIM // 2).astype(BF16)
    v_o[0] = pa[:, 768:1280].astype(BF16)

    ki = pa[:, 1280:1408]
    lane = lax.broadcasted_iota(I32, (tm, LANES), 1)
    real = lane < IDX_DIM
    mu = jnp.sum(jnp.where(real, ki, 0.0), axis=-1, keepdims=True) * (1.0 / IDX_DIM)
    xc = jnp.where(real, ki - mu, 0.0)
    var = jnp.sum(xc * xc, axis=-1, keepdims=True) * (1.0 / IDX_DIM)
    kn = xc * lax.rsqrt(var + NORM_EPS) * lng_ref[...] + lnb_ref[...]
    kr = _rope(kn, cosi, sini, IDX_ROPE_DIM // 2)
    ki_o[0] = jnp.where(real, kr, pltpu.roll(kr, IDX_DIM, 1)).astype(BF16)

    ps = _dot(h, wsm_ref[...])
    w_o[0] = ps[:, :LANES]
    beta_o[0] = jax.nn.sigmoid(ps[:, LANES:2 * LANES])
    xa = ps[:, 2 * LANES:] + dtb_ref[...]
    softplus = jnp.maximum(xa, 0.0) + jnp.log(1.0 + jnp.exp(-jnp.abs(xa)))
    g_o[0] = -jnp.exp(alog_ref[...]) * softplus

    pg = _dot(h, wgdn_ref[...])

    @pl.when(tile == 0)
    def _():
        pbuf[0:8, :] = jnp.zeros((8, pbuf.shape[1]), F32)

    @pl.when(tile > 0)
    def _():
        pbuf[0:8, :] = pbuf[tm:tm + 8, :]

    pbuf[8:tm + 8, :] = pg
    conv = pbuf[5:5 + tm, :] * conv_ref[0:1, :]
    conv = conv + pbuf[6:6 + tm, :] * conv_ref[1:2, :]
    conv = conv + pbuf[7:7 + tm, :] * conv_ref[2:3, :]
    conv = conv + pg * conv_ref[3:4, :]
    act = conv * jax.nn.sigmoid(conv)
    W = GDN_HEADS * GDN_HEAD_DIM
    for hd in range(GDN_HEADS):
        sl = slice(hd * GDN_HEAD_DIM, (hd + 1) * GDN_HEAD_DIM)
        xq = act[:, hd * GDN_HEAD_DIM:(hd + 1) * GDN_HEAD_DIM]
        xk = act[:, W + hd * GDN_HEAD_DIM:W + (hd + 1) * GDN_HEAD_DIM]
        qn = xq * lax.rsqrt(jnp.sum(xq * xq, axis=-1, keepdims=True) + NORM_EPS)
        gq_o[0, :, sl] = qn * (GDN_HEAD_DIM ** -0.5)
        gk_o[0, :, sl] = xk * lax.rsqrt(jnp.sum(xk * xk, axis=-1, keepdims=True) + NORM_EPS)
    gv_o[0] = act[:, 2 * W:]
    z_o[0] = _dot(h, wz_ref[...])


def _prep(x, pre_g, watt, wsm, wgdn, wz, qan, wqb, wqib, lng, lnb, convw, alog, dtb,
          cosa, sina, cosi, sini, *, tm):
    B, L, D = x.shape
    n = L // tm
    full = lambda a: pl.BlockSpec(a.shape, lambda b, i: (0, 0))
    tab = pl.BlockSpec((tm, LANES), lambda b, i: (i, 0))
    tok = lambda w: pl.BlockSpec((1, tm, w), lambda b, i: (b, i, 0))
    outs = [(512, BF16), (512, BF16), (512, BF16), (256, BF16), (LANES, BF16), (LANES, F32),
            (512, F32), (512, F32), (512, F32), (512, F32), (LANES, F32), (LANES, F32)]
    return pl.pallas_call(
        functools.partial(_prep_body, tm=tm),
        grid=(B, n),
        in_specs=[tok(D), full(pre_g), full(watt), full(wsm), full(wgdn), full(wz), full(qan),
                  full(wqb), full(wqib), full(lng), full(lnb), full(convw), full(alog), full(dtb),
                  tab, tab, tab, tab],
        out_specs=[tok(w) for w, _ in outs],
        out_shape=[jax.ShapeDtypeStruct((B, L, w), dt) for w, dt in outs],
        scratch_shapes=[pltpu.VMEM((tm + 8, wgdn.shape[1]), F32)],
        compiler_params=pltpu.CompilerParams(
            dimension_semantics=("arbitrary", "arbitrary"), vmem_limit_bytes=VMEM_LIMIT),
        name="prep",
    )(x, pre_g, watt, wsm, wgdn, wz, qan, wqb, wqib, lng, lnb, convw, alog, dtb,
      cosa, sina, cosi, sini)


def _attn_body(q_ref, qi_ref, w_ref, k_ref, v_ref, ki_ref, o_ref,
               key_scr, lo_scr, hi_scr, clo_scr, chi_scr, done_scr, m_scr, l_scr, acc_scr,
               *, TQ, KC, RB, L, top_k):
    i = pl.program_id(1)
    t0 = i * TQ
    nch = (t0 + TQ + KC - 1) // KC
    row = t0 + lax.broadcasted_iota(I32, (TQ, 1), 0)
    lane = lax.broadcasted_iota(I32, (TQ, LANES), 1)
    lower = lane < ATT_HEAD_DIM
    zero_bf = jnp.zeros((TQ, LANES), BF16)

    wv = w_ref[0]
    wcol = [jnp.sum(jnp.where(lane == hh, wv, 0.0), axis=1, keepdims=True) for hh in range(IDX_HEADS)]
    qim = []
    for hh in range(IDX_HEADS):
        slab = qi_ref[0, :, (hh // 2) * LANES:(hh // 2 + 1) * LANES]
        qim.append(jnp.where(lower if hh % 2 == 0 else ~lower, slab, zero_bf))
    cmr = lax.broadcasted_iota(I32, (TQ, KC), 1) - lax.broadcasted_iota(I32, (TQ, KC), 0)
    lo_scr[...] = jnp.full((TQ, 1), jnp.inf, F32)
    hi_scr[...] = jnp.full((TQ, 1), -jnp.inf, F32)

    def p1(j, carry):
        kk = ki_ref[0, pl.ds(pl.multiple_of(j * KC, KC), KC), :]
        sc = jnp.zeros((TQ, KC), F32)
        for hh in range(IDX_HEADS):
            sc = sc + wcol[hh] * jnp.maximum(_dot_nt(qim[hh], kk), 0.0)
        causal = cmr <= t0 - j * KC
        key_scr[j] = jnp.where(causal, sc, -jnp.inf)
        lo_scr[...] = jnp.minimum(lo_scr[...], jnp.min(jnp.where(causal, sc, jnp.inf), axis=1, keepdims=True))
        hi_scr[...] = jnp.maximum(hi_scr[...], jnp.max(jnp.where(causal, sc, -jnp.inf), axis=1, keepdims=True))
        return carry

    lax.fori_loop(0, nch, p1, 0)

    krow = jnp.minimum(top_k, row + 1).astype(F32)

    def count(pred):
        cs = []
        for r in range(TQ // RB):
            def body(j, acc, r=r):
                for g in range(KC // LANES):
                    blk = key_scr[j, r * RB:(r + 1) * RB, g * LANES:(g + 1) * LANES]
                    acc = acc + jnp.where(pred(blk, j * KC + g * LANES, r * RB), 1.0, 0.0)
                return acc
            acc = lax.fori_loop(0, nch, body, jnp.zeros((RB, LANES), F32))
            cs.append(jnp.sum(acc, axis=1, keepdims=True))
        return cs[0] if len(cs) == 1 else jnp.concatenate(cs, axis=0)

    def rows(vec, r0):
        return jnp.broadcast_to(vec[r0:r0 + RB], (RB, LANES))

    zero = jnp.zeros((TQ, 1), F32)
    ncausal = (row + 1).astype(F32)
    lo = lo_scr[...]
    top = hi_scr[...]
    hi = top + (jnp.abs(top) * (2.0 ** -20) + 1e-30)
    c_ge0 = count(lambda blk, c0, r0: blk >= rows(zero, r0))
    c_gt0 = count(lambda blk, c0, r0: blk > rows(zero, r0))
    all_rows = ncausal == krow
    at_zero = (c_gt0 < krow) & (c_ge0 >= krow) & ~all_rows
    up = (c_ge0 >= krow) & (lo < 0.0)
    down = (c_ge0 < krow) & (hi > 0.0)
    lo_scr[...] = jnp.where(up, 0.0, lo)
    clo_scr[...] = jnp.where(up, c_ge0, ncausal)
    hi_scr[...] = jnp.where(down, 0.0, hi)
    chi_scr[...] = jnp.where(at_zero, c_gt0, jnp.where(down, c_ge0, 0.0))
    done_scr[...] = jnp.where(at_zero | all_rows, 1.0, 0.0)

    def bisect(active):
        lo, hi, done = lo_scr[...], hi_scr[...], done_scr[...]
        mid = 0.5 * lo + 0.5 * hi
        stuck = (mid <= lo) | (mid >= hi)
        c = count(lambda blk, c0, r0: blk >= rows(mid, r0))
        ge = c >= krow
        upd = (done == 0.0) & ~stuck
        lo_scr[...] = jnp.where(upd & ge, mid, lo)
        clo_scr[...] = jnp.where(upd & ge, c, clo_scr[...])
        hi_scr[...] = jnp.where(upd & ~ge, mid, hi)
        chi_scr[...] = jnp.where(upd & ~ge, c, chi_scr[...])
        new_done = jnp.where(stuck | (upd & (c == krow)), 1.0, done)
        done_scr[...] = new_done
        return jnp.max(jnp.where(new_done == 0.0, 1.0, 0.0)).astype(I32)

    lax.while_loop(lambda a: a > 0, bisect, jnp.max(jnp.where(done_scr[...] == 0.0, 1.0, 0.0)).astype(I32))

    thr = lo_scr[...]
    tie = clo_scr[...] > krow
    need = krow - chi_scr[...]

    @pl.when(jnp.max(jnp.where(tie, 1.0, 0.0)) > 0.0)
    def _():
        lanei = lax.broadcasted_iota(I32, (RB, LANES), 1)

        def eq_le(pm):
            def pred(blk, c0, r0):
                thr_b = jnp.broadcast_to(thr[r0:r0 + RB], (RB, LANES))
                pm_b = jnp.broadcast_to(pm[r0:r0 + RB], (RB, LANES))
                return (blk == thr_b) & (lanei + c0 <= pm_b)
            return pred

        def step(_, carry):
            plo, phi = carry
            pm = (plo + phi) >> 1
            ok = count(eq_le(pm)) >= need
            return jnp.where(ok, plo, pm), jnp.where(ok, pm, phi)

        nbits = int(math.ceil(math.log2(L))) + 1
        _, cut = lax.fori_loop(0, nbits, step,
                               (jnp.full((TQ, 1), -1, I32), jnp.full((TQ, 1), L - 1, I32)))
        cut = jnp.where(tie, cut, L)
        col = lax.broadcasted_iota(I32, (TQ, KC), 1)

        def drop(j, carry):
            kc = key_scr[j]
            key_scr[j] = jnp.where((kc == thr) & (col + j * KC > cut), -jnp.inf, kc)
            return carry

        lax.fori_loop(0, nch, drop, 0)

    m_scr[...] = jnp.full(m_scr.shape, NEG_BIG, F32)
    l_scr[...] = jnp.zeros(l_scr.shape, F32)
    acc_scr[...] = jnp.zeros(acc_scr.shape, F32)
    qm = []
    for hh in range(ATT_HEADS):
        slab = q_ref[0, :, (hh // 2) * LANES:(hh // 2 + 1) * LANES]
        qm.append(jnp.where(lower if hh % 2 == 0 else ~lower, slab, zero_bf))

    def p3(j, carry):
        start = pl.multiple_of(j * KC, KC)
        bias = jnp.where(key_scr[j] >= thr, 0.0, NEG_BIG)
        for pr in range(ATT_HEADS // 2):
            kp = k_ref[0, pl.ds(start, KC), pr * LANES:(pr + 1) * LANES]
            vp = v_ref[0, pl.ds(start, KC), pr * LANES:(pr + 1) * LANES]
            alphas, pvs = [], []
            for hh in (2 * pr, 2 * pr + 1):
                s = _dot_nt(qm[hh], kp) + bias
                m_old = m_scr[hh]
                m_new = jnp.maximum(m_old, jnp.max(s, axis=1, keepdims=True))
                alpha = jnp.exp(m_old - m_new)
                p = jnp.exp(s - m_new)
                l_scr[hh] = alpha * l_scr[hh] + jnp.sum(p, axis=1, keepdims=True)
                m_scr[hh] = m_new
                alphas.append(alpha)
                pvs.append(_dot(p.astype(BF16), vp))
            acc_scr[pr] = (jnp.where(lower, alphas[0], alphas[1]) * acc_scr[pr]
                           + jnp.where(lower, pvs[0], pvs[1]))
        return carry

    lax.fori_loop(0, nch, p3, 0)
    for pr in range(ATT_HEADS // 2):
        lsum = jnp.where(lower, l_scr[2 * pr], l_scr[2 * pr + 1])
        o_ref[0, :, pr * LANES:(pr + 1) * LANES] = (acc_scr[pr] / lsum).astype(o_ref.dtype)


def _attention(q, qi, w, k, v, ki, *, TQ, KC, RB):
    B, L, _ = q.shape
    top_k = min(TOPK_MAX, L // 4)
    nq = L // TQ
    blk = lambda width: pl.BlockSpec((1, TQ, width), lambda b, i: (b, i, 0))
    whole = lambda width: pl.BlockSpec((1, L, width), lambda b, i: (b, 0, 0),
                                       pipeline_mode=pl.Buffered(1))
    return pl.pallas_call(
        functools.partial(_attn_body, TQ=TQ, KC=KC, RB=RB, L=L, top_k=top_k),
        grid=(B, nq),
        in_specs=[blk(512), blk(256), blk(LANES), whole(512), whole(512), whole(LANES)],
        out_specs=blk(512),
        out_shape=jax.ShapeDtypeStruct((B, L, 512), BF16),
        scratch_shapes=[
            pltpu.VMEM((L // KC, TQ, KC), F32),
            pltpu.VMEM((TQ, 1), F32), pltpu.VMEM((TQ, 1), F32),
            pltpu.VMEM((TQ, 1), F32), pltpu.VMEM((TQ, 1), F32), pltpu.VMEM((TQ, 1), F32),
            pltpu.VMEM((ATT_HEADS, TQ, 1), F32), pltpu.VMEM((ATT_HEADS, TQ, 1), F32),
            pltpu.VMEM((ATT_HEADS // 2, TQ, LANES), F32),
        ],
        compiler_params=pltpu.CompilerParams(
            dimension_semantics=("arbitrary", "arbitrary"), vmem_limit_bytes=VMEM_LIMIT),
        name="dsa_attention",
    )(q, qi, w, k, v, ki)


def _gdn_body(q_ref, k_ref, v_ref, z_ref, g_ref, b_ref, gn_ref, o_ref, s_scr):
    C = GDN_CHUNK

    @pl.when(pl.program_id(1) == 0)
    def _():
        s_scr[...] = jnp.zeros_like(s_scr)

    ri = lax.broadcasted_iota(I32, (C, C), 0)
    ci = lax.broadcasted_iota(I32, (C, C), 1)
    incl = ci <= ri
    strict = ci < ri
    eye = ci == ri
    lane = lax.broadcasted_iota(I32, (C, LANES), 1)
    gv, bv = g_ref[0], b_ref[0]
    for hd in range(GDN_HEADS):
        sl = slice(hd * GDN_HEAD_DIM, (hd + 1) * GDN_HEAD_DIM)
        q, k, v = q_ref[0, :, sl], k_ref[0, :, sl], v_ref[0, :, sl]
        g_col = jnp.sum(jnp.where(lane == hd, gv, 0.0), axis=1, keepdims=True)
        beta = jnp.sum(jnp.where(lane == hd, bv, 0.0), axis=1, keepdims=True)
        g_row = jnp.sum(jnp.where(eye, jnp.broadcast_to(g_col, (C, C)), 0.0), axis=0, keepdims=True)
        gc_col = jnp.sum(jnp.where(incl, jnp.broadcast_to(g_row, (C, C)), 0.0), axis=1, keepdims=True)
        gc_row = jnp.sum(jnp.where(ri <= ci, jnp.broadcast_to(g_col, (C, C)), 0.0), axis=0, keepdims=True)
        decay = jnp.where(incl, jnp.exp(jnp.where(incl, gc_col - gc_row, 0.0)), 0.0)
        eg = jnp.exp(gc_col)
        g_last = gc_col[C - 1:C, :]
        kb = k * beta
        a_mat = jnp.where(strict, _dot_nt(kb, k, HIGHEST) * decay, 0.0)
        t_inv = jnp.where(eye, 1.0, 0.0) - a_mat
        pw = a_mat
        for _ in range(5):
            pw = _dot_hi(pw, pw)
            t_inv = t_inv + _dot_hi(t_inv, pw)
        u = _dot_hi(t_inv, v * beta)
        w = _dot_hi(t_inv, kb * eg)
        qk = jnp.where(incl, _dot_nt(q, k, HIGHEST) * decay, 0.0)
        s_old = s_scr[hd]
        v_new = u - _dot_hi(w, s_old)
        o = _dot_hi(q * eg, s_old) + _dot_hi(qk, v_new)
        k_tail = k * jnp.exp(g_last - gc_col)
        s_scr[hd] = s_old * jnp.exp(g_last) + _dot_hi(k_tail.T, v_new)
        z = z_ref[0, :, sl]
        o_ref[0, :, sl] = (_rms(o, gn_ref[...]) * (z * jax.nn.sigmoid(z))).astype(o_ref.dtype)


def _gdn(gq, gk, gv, z, g, beta, gn):
    B, L, W = gq.shape
    C = GDN_CHUNK
    tok = lambda width: pl.BlockSpec((1, C, width), lambda b, n: (b, n, 0))
    return pl.pallas_call(
        _gdn_body,
        grid=(B, L // C),
        in_specs=[tok(W), tok(W), tok(W), tok(W), tok(LANES), tok(LANES),
                  pl.BlockSpec(gn.shape, lambda b, n: (0, 0))],
        out_specs=tok(W),
        out_shape=jax.ShapeDtypeStruct((B, L, W), BF16),
        scratch_shapes=[pltpu.VMEM((GDN_HEADS, GDN_HEAD_DIM, GDN_HEAD_DIM), F32)],
        compiler_params=pltpu.CompilerParams(
            dimension_semantics=("arbitrary", "arbitrary"), vmem_limit_bytes=VMEM_LIMIT),
        name="gdn",
    )(gq, gk, gv, z, g, beta, gn)


def _out_body(x_ref, a_ref, d_ref, wa_ref, wd_ref, g_ref, o_ref):
    m = _dot(a_ref[...], wa_ref[...]) + _dot(d_ref[...], wd_ref[...])
    o_ref[...] = x_ref[...] + _rms(m, g_ref[...])


def _out_proj(x, att, gdn, wa, wd, post_g, *, tm):
    T, D = x.shape
    W = att.shape[1]
    row = lambda width: pl.BlockSpec((tm, width), lambda i: (i, 0))
    full = lambda a: pl.BlockSpec(a.shape, lambda i: (0, 0))
    return pl.pallas_call(
        _out_body,
        grid=(T // tm,),
        in_specs=[row(D), row(W), row(W), full(wa), full(wd), full(post_g)],
        out_specs=row(D),
        out_shape=jax.ShapeDtypeStruct((T, D), F32),
        compiler_params=pltpu.CompilerParams(
            dimension_semantics=("parallel",), vmem_limit_bytes=VMEM_LIMIT),
        name="out_proj",
    )(x, att, gdn, wa, wd, post_g)


def _rope_tables(L, rot_dim):
    half = rot_dim // 2
    inv_freq = 1.0 / (ROPE_THETA ** (jnp.arange(half, dtype=F32) / half))
    ang = jnp.arange(L, dtype=jnp.int32).astype(F32)[:, None] * inv_freq[None, :]
    cos, sin = jnp.cos(ang), jnp.sin(ang)
    pad = 64 - rot_dim
    cos64 = jnp.concatenate([cos, cos, jnp.ones((L, pad), F32)], axis=1)
    sin64 = jnp.concatenate([-sin, sin, jnp.zeros((L, pad), F32)], axis=1)
    return jnp.tile(cos64, (1, 2)), jnp.tile(sin64, (1, 2))


def _pad_cols(a, width):
    return jnp.pad(a, ((0, 0), (0, width - a.shape[1])))


def _layer(x, p, *, tm_ffn, tf, tm_prep, tm_out, TQ, KC, RB):
    B, L, D = x.shape
    T = B * L
    row = lambda a: a.reshape(1, -1)
    x2 = _ffn(x.reshape(T, D), row(p['ffn1_pre_g']), row(p['ffn1_post_g']),
              p['ffn1_w_gate'].astype(BF16), p['ffn1_w_up'].astype(BF16), p['ffn1_w_down'].astype(BF16),
              tm=tm_ffn, tf=tf)

    w_in = p['w_in']
    sizes = (Q_LORA_RANK, 512, 512, IDX_DIM, IDX_HEADS, 512, 512, 512, 512, GDN_HEADS, GDN_HEADS)
    offs = [0]
    for s in sizes:
        offs.append(offs[-1] + s)
    col = lambda a, b: w_in[:, offs[a]:offs[b]]
    watt = jnp.concatenate([col(0, 3), _pad_cols(col(3, 4), LANES)], axis=1).astype(BF16)
    wsm = jnp.concatenate([_pad_cols(col(4, 5), LANES), _pad_cols(col(9, 10), LANES),
                           _pad_cols(col(10, 11), LANES)], axis=1).astype(BF16)
    wgdn = col(5, 8).astype(BF16)
    wz = col(8, 9).astype(BF16)
    cosa, sina = _rope_tables(L, ATT_HEAD_DIM)
    cosi, sini = _rope_tables(L, IDX_ROPE_DIM)
    q, k, v, qi, ki, w, gq, gk, gv, z, beta, g = _prep(
        x2.reshape(B, L, D), row(p['mix_pre_g']), watt, wsm, wgdn, wz, row(p['q_a_norm_g']),
        p['w_q_b'].astype(BF16), p['w_qidx_b'].astype(BF16),
        _pad_cols(row(p['kidx_ln_g']), LANES), _pad_cols(row(p['kidx_ln_b']), LANES),
        p['gdn_conv_w'], _pad_cols(row(p['gdn_a_log']), LANES), _pad_cols(row(p['gdn_dt_bias']), LANES),
        cosa, sina, cosi, sini, tm=tm_prep)

    att = _attention(q, qi, w, k, v, ki, TQ=TQ, KC=KC, RB=RB)
    gdn = _gdn(gq, gk, gv, z, g, beta, row(p['gdn_out_norm_g']))
    w_out = p['w_out'].astype(BF16)
    x3 = _out_proj(x2, att.reshape(T, -1), gdn.reshape(T, -1), w_out[:512], w_out[512:],
                   row(p['mix_post_g']), tm=tm_out)
    x4 = _ffn(x3, row(p['ffn2_pre_g']), row(p['ffn2_post_g']),
              p['ffn2_w_gate'].astype(BF16), p['ffn2_w_up'].astype(BF16), p['ffn2_w_down'].astype(BF16),
              tm=tm_ffn, tf=tf)
    return x4.reshape(B, L, D)


_NAMES = ('ffn1_pre_g', 'ffn1_post_g', 'ffn1_w_gate', 'ffn1_w_up', 'ffn1_w_down', 'mix_pre_g',
          'mix_post_g', 'w_in', 'q_a_norm_g', 'w_q_b', 'w_qidx_b', 'kidx_ln_g', 'kidx_ln_b',
          'gdn_conv_w', 'gdn_a_log', 'gdn_dt_bias', 'gdn_out_norm_g', 'w_out', 'ffn2_pre_g',
          'ffn2_post_g', 'ffn2_w_gate', 'ffn2_w_up', 'ffn2_w_down')


def kernel(x, ffn1_pre_g, ffn1_post_g, ffn1_w_gate, ffn1_w_up, ffn1_w_down, mix_pre_g, mix_post_g, w_in, q_a_norm_g, w_q_b, w_qidx_b, kidx_ln_g, kidx_ln_b, gdn_conv_w, gdn_a_log, gdn_dt_bias, gdn_out_norm_g, w_out, ffn2_pre_g, ffn2_post_g, ffn2_w_gate, ffn2_w_up, ffn2_w_down):
    args = (ffn1_pre_g, ffn1_post_g, ffn1_w_gate, ffn1_w_up, ffn1_w_down, mix_pre_g, mix_post_g,
            w_in, q_a_norm_g, w_q_b, w_qidx_b, kidx_ln_g, kidx_ln_b, gdn_conv_w, gdn_a_log,
            gdn_dt_bias, gdn_out_norm_g, w_out, ffn2_pre_g, ffn2_post_g, ffn2_w_gate, ffn2_w_up,
            ffn2_w_down)
    for layer in range(ffn1_pre_g.shape[0]):
        p = {n: a[layer] for n, a in zip(_NAMES, args)}
        L = x.shape[1]
        x = _layer(x, p, tm_ffn=min(1024, L), tf=1408, tm_prep=min(512, L), tm_out=min(1024, L),
                   TQ=min(256, L), KC=min(512, L), RB=64)
    return x
```

```python
import functools
import math

import jax
import jax.numpy as jnp
from jax import lax
from jax.experimental import pallas as pl
from jax.experimental.pallas import tpu as pltpu

F32 = jnp.float32
BF16 = jnp.bfloat16
I32 = jnp.int32

LANES = 128
ATT_HEAD_DIM = 64
ATT_HEADS = 8
IDX_HEADS = 4
IDX_DIM = 64
IDX_ROPE_DIM = 32
Q_LORA_RANK = 256
TOPK_MAX = 256
GDN_HEAD_DIM = 128
GDN_HEADS = 4
GDN_CONV = 4
GDN_CHUNK = 64
ROPE_THETA = 10000.0
NORM_EPS = 1e-6
NEG_BIG = -1e30

VMEM_LIMIT = 56 * 1024 * 1024


def _rms(x, g):
    return x * lax.rsqrt(jnp.mean(x * x, axis=-1, keepdims=True) + NORM_EPS) * g


def _dot(a, b):
    return jnp.dot(a, b, preferred_element_type=F32)


def _dot_nt(a, b):
    return lax.dot_general(a, b, (((1,), (1,)), ((), ())), preferred_element_type=F32)


def _ffn_body(x_ref, pre_ref, post_ref, wg_ref, wu_ref, wd_ref, o_ref, xn_ref, acc_ref):
    j = pl.program_id(1)

    @pl.when(j == 0)
    def _():
        xn_ref[...] = _rms(x_ref[...], pre_ref[...]).astype(BF16)
        acc_ref[...] = jnp.zeros_like(acc_ref)

    xn = xn_ref[...]
    g = _dot(xn, wg_ref[...])
    u = _dot(xn, wu_ref[...])
    h = (g * jax.nn.sigmoid(g) * u).astype(BF16)
    acc_ref[...] += _dot(h, wd_ref[...])

    @pl.when(j == pl.num_programs(1) - 1)
    def _():
        o_ref[...] = x_ref[...] + 0.5 * _rms(acc_ref[...], post_ref[...])


def _ffn(x, pre_g, post_g, wg, wu, wd, *, tm, tf):
    T, D = x.shape
    F = wg.shape[1]
    return pl.pallas_call(
        _ffn_body,
        grid=(T // tm, F // tf),
        in_specs=[
            pl.BlockSpec((tm, D), lambda i, j: (i, 0)),
            pl.BlockSpec((1, D), lambda i, j: (0, 0)),
            pl.BlockSpec((1, D), lambda i, j: (0, 0)),
            pl.BlockSpec((D, tf), lambda i, j: (0, j)),
            pl.BlockSpec((D, tf), lambda i, j: (0, j)),
            pl.BlockSpec((tf, D), lambda i, j: (j, 0)),
        ],
        out_specs=pl.BlockSpec((tm, D), lambda i, j: (i, 0)),
        out_shape=jax.ShapeDtypeStruct((T, D), F32),
        scratch_shapes=[pltpu.VMEM((tm, D), BF16), pltpu.VMEM((tm, D), F32)],
        compiler_params=pltpu.CompilerParams(
            dimension_semantics=("parallel", "arbitrary"), vmem_limit_bytes=VMEM_LIMIT),
        name="ffn",
    )(x, pre_g, post_g, wg, wu, wd)


def _rope(x, cos, sin_signed, half):
    n, W = x.shape
    lane = lax.broadcasted_iota(I32, (n, LANES), 1)
    first = (lane & 63) < half
    outs = []
    for s in range(W // LANES):
        xs = x[:, s * LANES:(s + 1) * LANES]
        partner = jnp.where(first, pltpu.roll(xs, LANES - half, 1), pltpu.roll(xs, half, 1))
        outs.append(xs * cos + partner * sin_signed)
    return outs[0] if len(outs) == 1 else jnp.concatenate(outs, axis=1)


def _prep_body(x_ref, pre_ref, watt_ref, wsm_ref, wgdn_ref, wz_ref, qan_ref, wqb_ref, wqib_ref,
               lng_ref, lnb_ref, conv_ref, alog_ref, dtb_ref, cosa_ref, sina_ref, cosi_ref, sini_ref,
               q_o, k_o, vt_o, qi_o, ki_o, w_o, gq_o, gk_o, gv_o, z_o, beta_o, g_o,
               pbuf, *, tm):
    tile = pl.program_id(1)
    h = _rms(x_ref[0], pre_ref[...]).astype(BF16)
    cosa, sina = cosa_ref[...], sina_ref[...]
    cosi, sini = cosi_ref[...], sini_ref[...]

    pa = _dot(h, watt_ref[...])
    cq = _rms(pa[:, :Q_LORA_RANK], qan_ref[...]).astype(BF16)
    q = _rope(_dot(cq, wqb_ref[...]), cosa, sina, ATT_HEAD_DIM // 2)
    q_o[0] = (q * (ATT_HEAD_DIM ** -0.5)).astype(BF16)
    qi_o[0] = _rope(_dot(cq, wqib_ref[...]), cosi, sini, IDX_ROPE_DIM // 2).astype(BF16)
    k_o[0] = _rope(pa[:, 256:768], cosa, sina, ATT_HEAD_DIM // 2).astype(BF16)
    vt_o[0, 0] = pa[:, 768:1280].T.astype(BF16)

    ki = pa[:, 1280:1408]
    lane = lax.broadcasted_iota(I32, (tm, LANES), 1)
    real = lane < IDX_DIM
    mu = jnp.sum(jnp.where(real, ki, 0.0), axis=-1, keepdims=True) * (1.0 / IDX_DIM)
    xc = jnp.where(real, ki - mu, 0.0)
    var = jnp.sum(xc * xc, axis=-1, keepdims=True) * (1.0 / IDX_DIM)
    kn = xc * lax.rsqrt(var + NORM_EPS) * lng_ref[...] + lnb_ref[...]
    kr = _rope(kn, cosi, sini, IDX_ROPE_DIM // 2)
    ki_o[0] = jnp.where(real, kr, pltpu.roll(kr, IDX_DIM, 1)).astype(BF16)

    ps = _dot(h, wsm_ref[...])
    w_o[0] = ps[:, :LANES]
    beta_o[0] = jax.nn.sigmoid(ps[:, LANES:2 * LANES])
    xa = ps[:, 2 * LANES:] + dtb_ref[...]
    softplus = jnp.maximum(xa, 0.0) + jnp.log(1.0 + jnp.exp(-jnp.abs(xa)))
    g_o[0] = -jnp.exp(alog_ref[...]) * softplus

    pg = _dot(h, wgdn_ref[...])

    @pl.when(tile == 0)
    def _():
        pbuf[0:8, :] = jnp.zeros((8, pbuf.shape[1]), F32)

    @pl.when(tile > 0)
    def _():
        pbuf[0:8, :] = pbuf[tm:tm + 8, :]

    pbuf[8:tm + 8, :] = pg
    conv = pbuf[5:5 + tm, :] * conv_ref[0:1, :]
    conv = conv + pbuf[6:6 + tm, :] * conv_ref[1:2, :]
    conv = conv + pbuf[7:7 + tm, :] * conv_ref[2:3, :]
    conv = conv + pg * conv_ref[3:4, :]
    act = conv * jax.nn.sigmoid(conv)
    W = GDN_HEADS * GDN_HEAD_DIM
    for hd in range(GDN_HEADS):
        sl = slice(hd * GDN_HEAD_DIM, (hd + 1) * GDN_HEAD_DIM)
        xq = act[:, hd * GDN_HEAD_DIM:(hd + 1) * GDN_HEAD_DIM]
        xk = act[:, W + hd * GDN_HEAD_DIM:W + (hd + 1) * GDN_HEAD_DIM]
        qn = xq * lax.rsqrt(jnp.sum(xq * xq, axis=-1, keepdims=True) + NORM_EPS)
        gq_o[0, :, sl] = qn * (GDN_HEAD_DIM ** -0.5)
        gk_o[0, :, sl] = xk * lax.rsqrt(jnp.sum(xk * xk, axis=-1, keepdims=True) + NORM_EPS)
    gv_o[0] = act[:, 2 * W:]
    z_o[0] = _dot(h, wz_ref[...])


def _prep(x, pre_g, watt, wsm, wgdn, wz, qan, wqb, wqib, lng, lnb, convw, alog, dtb,
          cosa, sina, cosi, sini, *, tm):
    B, L, D = x.shape
    n = L // tm
    full = lambda a: pl.BlockSpec(a.shape, lambda b, i: (0, 0))
    tab = pl.BlockSpec((tm, LANES), lambda b, i: (i, 0))
    tok = lambda w: pl.BlockSpec((1, tm, w), lambda b, i: (b, i, 0))
    vt_spec = pl.BlockSpec((1, 1, 512, tm), lambda b, i: (b, i, 0, 0))
    outs = [(512, BF16), (512, BF16), (512, BF16), (256, BF16), (LANES, BF16), (LANES, F32),
            (512, F32), (512, F32), (512, F32), (512, F32), (LANES, F32), (LANES, F32)]
    return pl.pallas_call(
        functools.partial(_prep_body, tm=tm),
        grid=(B, n),
        in_specs=[tok(D), full(pre_g), full(watt), full(wsm), full(wgdn), full(wz), full(qan),
                  full(wqb), full(wqib), full(lng), full(lnb), full(convw), full(alog), full(dtb),
                  tab, tab, tab, tab],
        out_specs=[vt_spec if idx == 2 else tok(w) for idx, (w, _) in enumerate(outs)],
        out_shape=[jax.ShapeDtypeStruct((B, n, 512, tm) if idx == 2 else (B, L, w), dt)
                   for idx, (w, dt) in enumerate(outs)],
        scratch_shapes=[pltpu.VMEM((tm + 8, wgdn.shape[1]), F32)],
        compiler_params=pltpu.CompilerParams(
            dimension_semantics=("arbitrary", "arbitrary"), vmem_limit_bytes=VMEM_LIMIT),
        name="prep",
    )(x, pre_g, watt, wsm, wgdn, wz, qan, wqb, wqib, lng, lnb, convw, alog, dtb,
      cosa, sina, cosi, sini)


def _fold8(x, op, ways=4):
    parts = [x[r:r + 8] for r in range(0, x.shape[0], 8)]
    accs = parts[:ways]
    for idx, part in enumerate(parts[ways:]):
        accs[idx % ways] = op(accs[idx % ways], part)
    while len(accs) > 1:
        accs = [op(a, b) for a, b in zip(accs[0::2], accs[1::2])] + ([accs[-1]] if len(accs) % 2 else [])
    return accs[0]


def _attn_body(q_ref, qi_ref, w_ref, k_ref, vt_ref, ki_ref, o_ref,
               sc_scr, st_scr, m_scr, l_scr, acc_scr, *, TQ, KC, L, top_k):
    i = pl.program_id(1)
    t0 = i * TQ
    nch = (t0 + TQ + KC - 1) // KC
    qpos = t0 + lax.broadcasted_iota(I32, (1, TQ), 1)
    lane = lax.broadcasted_iota(I32, (TQ, LANES), 1)
    lower = lane < ATT_HEAD_DIM
    zero_bf = jnp.zeros((TQ, LANES), BF16)
    rmc = lax.broadcasted_iota(I32, (KC, TQ), 0) - lax.broadcasted_iota(I32, (KC, TQ), 1)

    def head_halves(ref, n_heads):
        out = []
        for hh in range(n_heads):
            slab = ref[0, :, (hh // 2) * LANES:(hh // 2 + 1) * LANES]
            out.append(jnp.where(lower if hh % 2 == 0 else ~lower, slab, zero_bf))
        return out

    wt = w_ref[0].T
    wrow = [wt[hh:hh + 1, :] for hh in range(IDX_HEADS)]
    qim = head_halves(qi_ref, IDX_HEADS)

    def p1(j, carry):
        lo, hi = carry
        kk = ki_ref[0, pl.ds(pl.multiple_of(j * KC, KC), KC), :]
        sc = jnp.zeros((KC, TQ), F32)
        for hh in range(IDX_HEADS):
            sc = sc + wrow[hh] * jnp.maximum(_dot_nt(kk, qim[hh]), 0.0)
        masked = jnp.where(rmc <= t0 - j * KC, sc, -jnp.inf)
        sc_scr[j] = masked
        lo = jnp.minimum(lo, jnp.min(_fold8(sc, jnp.minimum), axis=0, keepdims=True))
        hi = jnp.maximum(hi, jnp.max(_fold8(masked, jnp.maximum), axis=0, keepdims=True))
        return lo, hi

    lo, top = lax.fori_loop(0, nch, p1, (jnp.full((1, TQ), jnp.inf, F32), jnp.full((1, TQ), -jnp.inf, F32)))

    krow = jnp.minimum(top_k, qpos + 1).astype(F32)

    def count(pred):
        def body(j, acc):
            return acc + _fold8(jnp.where(pred(sc_scr[j], j * KC), 1.0, 0.0), jnp.add)
        acc = lax.fori_loop(0, nch, body, jnp.zeros((8, TQ), F32))
        return jnp.sum(acc, axis=0, keepdims=True)

    ncausal = (qpos + 1).astype(F32)
    hi = top + (jnp.abs(top) * (2.0 ** -20) + 1e-30)
    c_ge0 = count(lambda blk, k0: blk >= 0.0)
    c_gt0 = count(lambda blk, k0: blk > 0.0)
    all_rows = ncausal == krow
    at_zero = (c_gt0 < krow) & (c_ge0 >= krow) & ~all_rows
    up = (c_ge0 >= krow) & (lo < 0.0)
    down = (c_ge0 < krow) & (hi > 0.0)
    LO, HI, CLO, CHI, DONE = range(5)
    st_scr[LO:LO + 1, :] = jnp.where(up, 0.0, lo)
    st_scr[CLO:CLO + 1, :] = jnp.where(up, c_ge0, ncausal)
    st_scr[HI:HI + 1, :] = jnp.where(down, 0.0, hi)
    st_scr[CHI:CHI + 1, :] = jnp.where(at_zero, c_gt0, jnp.where(down, c_ge0, 0.0))
    done0 = jnp.where(at_zero | all_rows, 1.0, 0.0)
    st_scr[DONE:DONE + 1, :] = done0

    def bisect(active):
        lo, hi, done = st_scr[LO:LO + 1, :], st_scr[HI:HI + 1, :], st_scr[DONE:DONE + 1, :]
        mid = 0.5 * lo + 0.5 * hi
        stuck = (mid <= lo) | (mid >= hi)
        c = count(lambda blk, k0: blk >= mid)
        ge = c >= krow
        upd = (done == 0.0) & ~stuck
        st_scr[LO:LO + 1, :] = jnp.where(upd & ge, mid, lo)
        st_scr[CLO:CLO + 1, :] = jnp.where(upd & ge, c, st_scr[CLO:CLO + 1, :])
        st_scr[HI:HI + 1, :] = jnp.where(upd & ~ge, mid, hi)
        st_scr[CHI:CHI + 1, :] = jnp.where(upd & ~ge, c, st_scr[CHI:CHI + 1, :])
        new_done = jnp.where(stuck | (upd & (c == krow)), 1.0, done)
        st_scr[DONE:DONE + 1, :] = new_done
        return jnp.max(jnp.where(new_done == 0.0, 1.0, 0.0)).astype(I32)

    lax.while_loop(lambda a: a > 0, bisect, jnp.max(jnp.where(done0 == 0.0, 1.0, 0.0)).astype(I32))

    thr = st_scr[LO:LO + 1, :]
    tie = st_scr[CLO:CLO + 1, :] > krow
    need = krow - st_scr[CHI:CHI + 1, :]

    @pl.when(jnp.max(jnp.where(tie, 1.0, 0.0)) > 0.0)
    def _():
        kidx = lax.broadcasted_iota(I32, (KC, TQ), 0)

        def step(_, carry):
            plo, phi = carry
            pm = (plo + phi) >> 1
            ok = count(lambda blk, k0: (blk == thr) & (kidx + k0 <= pm)) >= need
            return jnp.where(ok, plo, pm), jnp.where(ok, pm, phi)

        nbits = int(math.ceil(math.log2(L))) + 1
        _, cut = lax.fori_loop(0, nbits, step,
                               (jnp.full((1, TQ), -1, I32), jnp.full((1, TQ), L - 1, I32)))
        cut = jnp.where(tie, cut, L)

        def drop(j, carry):
            blk = sc_scr[j]
            sc_scr[j] = jnp.where((blk == thr) & (kidx + j * KC > cut), -jnp.inf, blk)
            return carry

        lax.fori_loop(0, nch, drop, 0)

    m_scr[...] = jnp.full(m_scr.shape, NEG_BIG, F32)
    l_scr[...] = jnp.zeros(l_scr.shape, F32)
    acc_scr[...] = jnp.zeros(acc_scr.shape, F32)
    qm = head_halves(q_ref, ATT_HEADS)
    HD = ATT_HEAD_DIM

    heads = range(ATT_HEADS)

    def p3(j, carry):
        start = pl.multiple_of(j * KC, KC)
        bias = jnp.where(sc_scr[j] >= thr, 0.0, NEG_BIG)
        m_old = [m_scr[hh:hh + 1, :] for hh in heads]
        l_old = [l_scr[hh:hh + 1, :] for hh in heads]
        acc_old = [acc_scr[hh] for hh in heads]
        kp = [k_ref[0, pl.ds(start, KC), pr * LANES:(pr + 1) * LANES] for pr in range(ATT_HEADS // 2)]
        vt = [vt_ref[0, j, hh * HD:(hh + 1) * HD, :] for hh in heads]
        s = [_dot_nt(kp[hh // 2], qm[hh]) + bias for hh in heads]
        m_new = [jnp.maximum(m_old[hh], jnp.max(_fold8(s[hh], jnp.maximum), axis=0, keepdims=True))
                 for hh in heads]
        alpha = [jnp.exp(m_old[hh] - m_new[hh]) for hh in heads]
        p = [jnp.exp(s[hh] - m_new[hh]) for hh in heads]
        l_new = [alpha[hh] * l_old[hh] + jnp.sum(_fold8(p[hh], jnp.add), axis=0, keepdims=True)
                 for hh in heads]
        pv = [_dot(vt[hh], p[hh].astype(BF16)) for hh in heads]
        for hh in heads:
            m_scr[hh:hh + 1, :] = m_new[hh]
            l_scr[hh:hh + 1, :] = l_new[hh]
            acc_scr[hh] = alpha[hh] * acc_old[hh] + pv[hh]
        return carry

    lax.fori_loop(0, nch, p3, 0)
    for pr in range(ATT_HEADS // 2):
        halves = [acc_scr[hh] / l_scr[hh:hh + 1, :] for hh in (2 * pr, 2 * pr + 1)]
        o_ref[0, :, pr * LANES:(pr + 1) * LANES] = jnp.concatenate(halves, axis=0).T.astype(o_ref.dtype)


def _attention(q, qi, w, k, vt, ki, *, TQ, KC):
    B, L, _ = q.shape
    top_k = min(TOPK_MAX, L // 4)
    nq = L // TQ
    blk = lambda width: pl.BlockSpec((1, TQ, width), lambda b, i: (b, i, 0))
    whole = lambda width: pl.BlockSpec((1, L, width), lambda b, i: (b, 0, 0),
                                       pipeline_mode=pl.Buffered(1))
    vt_spec = pl.BlockSpec((1, L // KC, 512, KC), lambda b, i: (b, 0, 0, 0), pipeline_mode=pl.Buffered(1))
    return pl.pallas_call(
        functools.partial(_attn_body, TQ=TQ, KC=KC, L=L, top_k=top_k),
        grid=(B, nq),
        in_specs=[blk(512), blk(256), blk(LANES), whole(512), vt_spec, whole(LANES)],
        out_specs=blk(512),
        out_shape=jax.ShapeDtypeStruct((B, L, 512), BF16),
        scratch_shapes=[
            pltpu.VMEM((L // KC, KC, TQ), F32),
            pltpu.VMEM((8, TQ), F32),
            pltpu.VMEM((ATT_HEADS, TQ), F32), pltpu.VMEM((ATT_HEADS, TQ), F32),
            pltpu.VMEM((ATT_HEADS, ATT_HEAD_DIM, TQ), F32),
        ],
        compiler_params=pltpu.CompilerParams(
            dimension_semantics=("arbitrary", "arbitrary"), vmem_limit_bytes=VMEM_LIMIT),
        name="dsa_attention",
    )(q, qi, w, k, vt, ki)


def _mm(a, b):
    return jnp.dot(a.astype(BF16), b.astype(BF16), preferred_element_type=F32)


def _gdn_body(q_ref, k_ref, v_ref, z_ref, g_ref, b_ref, gn_ref, o_ref, s_scr, *, nb):
    C = GDN_CHUNK

    @pl.when(pl.program_id(0) == 0)
    def _():
        s_scr[...] = jnp.zeros_like(s_scr)

    ri = lax.broadcasted_iota(I32, (C, C), 0)
    ci = lax.broadcasted_iota(I32, (C, C), 1)
    incl = ci <= ri
    strict = ci < ri
    eye = ci == ri
    lane = lax.broadcasted_iota(I32, (C, LANES), 1)
    chains = [(b, hd) for b in range(nb) for hd in range(GDN_HEADS)]
    n = range(len(chains))
    sls = [slice(hd * GDN_HEAD_DIM, (hd + 1) * GDN_HEAD_DIM) for _, hd in chains]
    q = [q_ref[b, :, sl] for (b, _), sl in zip(chains, sls)]
    k = [k_ref[b, :, sl] for (b, _), sl in zip(chains, sls)]
    v = [v_ref[b, :, sl] for (b, _), sl in zip(chains, sls)]
    g_col = [jnp.sum(jnp.where(lane == hd, g_ref[b], 0.0), axis=1, keepdims=True) for b, hd in chains]
    beta = [jnp.sum(jnp.where(lane == hd, b_ref[b], 0.0), axis=1, keepdims=True) for b, hd in chains]
    g_row = [jnp.sum(jnp.where(eye, jnp.broadcast_to(x, (C, C)), 0.0), axis=0, keepdims=True) for x in g_col]
    gc_col = [jnp.sum(jnp.where(incl, jnp.broadcast_to(x, (C, C)), 0.0), axis=1, keepdims=True) for x in g_row]
    gc_row = [jnp.sum(jnp.where(ri <= ci, jnp.broadcast_to(x, (C, C)), 0.0), axis=0, keepdims=True) for x in g_col]
    decay = [jnp.where(incl, jnp.exp(jnp.where(incl, gc_col[c] - gc_row[c], 0.0)), 0.0) for c in n]
    eg = [jnp.exp(x) for x in gc_col]
    g_last = [x[C - 1:C, :] for x in gc_col]
    kb = [k[c] * beta[c] for c in n]
    k_bf = [x.astype(BF16) for x in k]
    a_mat = [jnp.where(strict, _dot_nt(kb[c].astype(BF16), k_bf[c]) * decay[c], 0.0) for c in n]
    qk = [jnp.where(incl, _dot_nt(q[c].astype(BF16), k_bf[c]) * decay[c], 0.0) for c in n]
    t_inv = [jnp.where(eye, 1.0, 0.0) - a for a in a_mat]
    pw = a_mat
    for _ in range(5):
        pw = [_mm(x, x) for x in pw]
        t_inv = [t_inv[c] + _mm(t_inv[c], pw[c]) for c in n]
    u = [_mm(t_inv[c], v[c] * beta[c]) for c in n]
    w = [_mm(t_inv[c], kb[c] * eg[c]) for c in n]
    s_old = [s_scr[c] for c in n]
    v_new = [u[c] - _mm(w[c], s_old[c]) for c in n]
    qs = [_mm(q[c] * eg[c], s_old[c]) for c in n]
    for c in n:
        k_tail = k[c] * jnp.exp(g_last[c] - gc_col[c])
        s_scr[c] = s_old[c] * jnp.exp(g_last[c]) + _mm(k_tail.T, v_new[c])
    for c, ((b, _), sl) in enumerate(zip(chains, sls)):
        o = qs[c] + _mm(qk[c], v_new[c])
        z = z_ref[b, :, sl]
        o_ref[b, :, sl] = (_rms(o, gn_ref[...]) * (z * jax.nn.sigmoid(z))).astype(o_ref.dtype)


def _gdn(gq, gk, gv, z, g, beta, gn):
    B, L, W = gq.shape
    C = GDN_CHUNK
    tok = lambda width: pl.BlockSpec((B, C, width), lambda n: (0, n, 0))
    return pl.pallas_call(
        functools.partial(_gdn_body, nb=B),
        grid=(L // C,),
        in_specs=[tok(W), tok(W), tok(W), tok(W), tok(LANES), tok(LANES),
                  pl.BlockSpec(gn.shape, lambda n: (0, 0))],
        out_specs=tok(W),
        out_shape=jax.ShapeDtypeStruct((B, L, W), BF16),
        scratch_shapes=[pltpu.VMEM((B * GDN_HEADS, GDN_HEAD_DIM, GDN_HEAD_DIM), F32)],
        compiler_params=pltpu.CompilerParams(
            dimension_semantics=("arbitrary",), vmem_limit_bytes=VMEM_LIMIT),
        name="gdn",
    )(gq, gk, gv, z, g, beta, gn)


def _out_body(x_ref, a_ref, d_ref, wa_ref, wd_ref, g_ref, o_ref):
    m = _dot(a_ref[...], wa_ref[...]) + _dot(d_ref[...], wd_ref[...])
    o_ref[...] = x_ref[...] + _rms(m, g_ref[...])


def _out_proj(x, att, gdn, wa, wd, post_g, *, tm):
    T, D = x.shape
    W = att.shape[1]
    row = lambda width: pl.BlockSpec((tm, width), lambda i: (i, 0))
    full = lambda a: pl.BlockSpec(a.shape, lambda i: (0, 0))
    return pl.pallas_call(
        _out_body,
        grid=(T // tm,),
        in_specs=[row(D), row(W), row(W), full(wa), full(wd), full(post_g)],
        out_specs=row(D),
        out_shape=jax.ShapeDtypeStruct((T, D), F32),
        compiler_params=pltpu.CompilerParams(
            dimension_semantics=("parallel",), vmem_limit_bytes=VMEM_LIMIT),
        name="out_proj",
    )(x, att, gdn, wa, wd, post_g)


def _rope_tables(L, rot_dim):
    half = rot_dim // 2
    inv_freq = 1.0 / (ROPE_THETA ** (jnp.arange(half, dtype=F32) / half))
    ang = jnp.arange(L, dtype=jnp.int32).astype(F32)[:, None] * inv_freq[None, :]
    cos, sin = jnp.cos(ang), jnp.sin(ang)
    pad = 64 - rot_dim
    cos64 = jnp.concatenate([cos, cos, jnp.ones((L, pad), F32)], axis=1)
    sin64 = jnp.concatenate([-sin, sin, jnp.zeros((L, pad), F32)], axis=1)
    return jnp.tile(cos64, (1, 2)), jnp.tile(sin64, (1, 2))


def _pad_cols(a, width):
    return jnp.pad(a, ((0, 0), (0, width - a.shape[1])))


def _layer(x, p, *, tm_ffn, tf, tm_out, TQ, KC):
    B, L, D = x.shape
    T = B * L
    row = lambda a: a.reshape(1, -1)
    x2 = _ffn(x.reshape(T, D), row(p['ffn1_pre_g']), row(p['ffn1_post_g']),
              p['ffn1_w_gate'].astype(BF16), p['ffn1_w_up'].astype(BF16), p['ffn1_w_down'].astype(BF16),
              tm=tm_ffn, tf=tf)

    w_in = p['w_in']
    sizes = (Q_LORA_RANK, 512, 512, IDX_DIM, IDX_HEADS, 512, 512, 512, 512, GDN_HEADS, GDN_HEADS)
    offs = [0]
    for s in sizes:
        offs.append(offs[-1] + s)
    col = lambda a, b: w_in[:, offs[a]:offs[b]]
    watt = jnp.concatenate([col(0, 3), _pad_cols(col(3, 4), LANES)], axis=1).astype(BF16)
    wsm = jnp.concatenate([_pad_cols(col(4, 5), LANES), _pad_cols(col(9, 10), LANES),
                           _pad_cols(col(10, 11), LANES)], axis=1).astype(BF16)
    wgdn = col(5, 8).astype(BF16)
    wz = col(8, 9).astype(BF16)
    cosa, sina = _rope_tables(L, ATT_HEAD_DIM)
    cosi, sini = _rope_tables(L, IDX_ROPE_DIM)
    q, k, vt, qi, ki, w, gq, gk, gv, z, beta, g = _prep(
        x2.reshape(B, L, D), row(p['mix_pre_g']), watt, wsm, wgdn, wz, row(p['q_a_norm_g']),
        p['w_q_b'].astype(BF16), p['w_qidx_b'].astype(BF16),
        _pad_cols(row(p['kidx_ln_g']), LANES), _pad_cols(row(p['kidx_ln_b']), LANES),
        p['gdn_conv_w'], _pad_cols(row(p['gdn_a_log']), LANES), _pad_cols(row(p['gdn_dt_bias']), LANES),
        cosa, sina, cosi, sini, tm=KC)

    att = _attention(q, qi, w, k, vt, ki, TQ=TQ, KC=KC)
    gdn = _gdn(gq, gk, gv, z, g, beta, row(p['gdn_out_norm_g']))
    w_out = p['w_out'].astype(BF16)
    x3 = _out_proj(x2, att.reshape(T, -1), gdn.reshape(T, -1), w_out[:512], w_out[512:],
                   row(p['mix_post_g']), tm=tm_out)
    x4 = _ffn(x3, row(p['ffn2_pre_g']), row(p['ffn2_post_g']),
              p['ffn2_w_gate'].astype(BF16), p['ffn2_w_up'].astype(BF16), p['ffn2_w_down'].astype(BF16),
              tm=tm_ffn, tf=tf)
    return x4.reshape(B, L, D)


_NAMES = ('ffn1_pre_g', 'ffn1_post_g', 'ffn1_w_gate', 'ffn1_w_up', 'ffn1_w_down', 'mix_pre_g',
          'mix_post_g', 'w_in', 'q_a_norm_g', 'w_q_b', 'w_qidx_b', 'kidx_ln_g', 'kidx_ln_b',
          'gdn_conv_w', 'gdn_a_log', 'gdn_dt_bias', 'gdn_out_norm_g', 'w_out', 'ffn2_pre_g',
          'ffn2_post_g', 'ffn2_w_gate', 'ffn2_w_up', 'ffn2_w_down')


def kernel(x, ffn1_pre_g, ffn1_post_g, ffn1_w_gate, ffn1_w_up, ffn1_w_down, mix_pre_g, mix_post_g, w_in, q_a_norm_g, w_q_b, w_qidx_b, kidx_ln_g, kidx_ln_b, gdn_conv_w, gdn_a_log, gdn_dt_bias, gdn_out_norm_g, w_out, ffn2_pre_g, ffn2_post_g, ffn2_w_gate, ffn2_w_up, ffn2_w_down):
    args = (ffn1_pre_g, ffn1_post_g, ffn1_w_gate, ffn1_w_up, ffn1_w_down, mix_pre_g, mix_post_g,
            w_in, q_a_norm_g, w_q_b, w_qidx_b, kidx_ln_g, kidx_ln_b, gdn_conv_w, gdn_a_log,
            gdn_dt_bias, gdn_out_norm_g, w_out, ffn2_pre_g, ffn2_post_g, ffn2_w_gate, ffn2_w_up,
            ffn2_w_down)
    for layer in range(ffn1_pre_g.shape[0]):
        p = {n: a[layer] for n, a in zip(_NAMES, args)}
        L = x.shape[1]
        x = _layer(x, p, tm_ffn=min(1024, L), tf=1408, tm_out=min(1024, L),
                   TQ=min(256, L), KC=min(512, L))
    return x
```

```python
import functools
import math

import jax
import jax.numpy as jnp
from jax import lax
from jax.experimental import pallas as pl
from jax.experimental.pallas import tpu as pltpu

F32 = jnp.float32
BF16 = jnp.bfloat16
I32 = jnp.int32

LANES = 128
ATT_HEAD_DIM = 64
ATT_HEADS = 8
IDX_HEADS = 4
IDX_DIM = 64
IDX_ROPE_DIM = 32
Q_LORA_RANK = 256
TOPK_MAX = 256
GDN_HEAD_DIM = 128
GDN_HEADS = 4
GDN_CONV = 4
GDN_CHUNK = 64
ROPE_THETA = 10000.0
NORM_EPS = 1e-6
NEG_BIG = -1e30

VMEM_LIMIT = 56 * 1024 * 1024


def _rms(x, g):
    return x * lax.rsqrt(jnp.mean(x * x, axis=-1, keepdims=True) + NORM_EPS) * g


def _dot(a, b):
    return jnp.dot(a, b, preferred_element_type=F32)


def _dot_nt(a, b):
    return lax.dot_general(a, b, (((1,), (1,)), ((), ())), preferred_element_type=F32)


def _ffn_body(x_ref, pre_ref, post_ref, wg_ref, wu_ref, wd_ref, o_ref, xn_ref, acc_ref):
    j = pl.program_id(1)

    @pl.when(j == 0)
    def _():
        xn_ref[...] = _rms(x_ref[...], pre_ref[...]).astype(BF16)
        acc_ref[...] = jnp.zeros_like(acc_ref)

    xn = xn_ref[...]
    g = _dot(xn, wg_ref[...])
    u = _dot(xn, wu_ref[...])
    h = (g * jax.nn.sigmoid(g) * u).astype(BF16)
    acc_ref[...] += _dot(h, wd_ref[...])

    @pl.when(j == pl.num_programs(1) - 1)
    def _():
        o_ref[...] = x_ref[...] + 0.5 * _rms(acc_ref[...], post_ref[...])


def _ffn(x, pre_g, post_g, wg, wu, wd, *, tm, tf):
    T, D = x.shape
    F = wg.shape[1]
    return pl.pallas_call(
        _ffn_body,
        grid=(T // tm, F // tf),
        in_specs=[
            pl.BlockSpec((tm, D), lambda i, j: (i, 0)),
            pl.BlockSpec((1, D), lambda i, j: (0, 0)),
            pl.BlockSpec((1, D), lambda i, j: (0, 0)),
            pl.BlockSpec((D, tf), lambda i, j: (0, j)),
            pl.BlockSpec((D, tf), lambda i, j: (0, j)),
            pl.BlockSpec((tf, D), lambda i, j: (j, 0)),
        ],
        out_specs=pl.BlockSpec((tm, D), lambda i, j: (i, 0)),
        out_shape=jax.ShapeDtypeStruct((T, D), F32),
        scratch_shapes=[pltpu.VMEM((tm, D), BF16), pltpu.VMEM((tm, D), F32)],
        compiler_params=pltpu.CompilerParams(
            dimension_semantics=("parallel", "arbitrary"), vmem_limit_bytes=VMEM_LIMIT),
        name="ffn",
    )(x, pre_g, post_g, wg, wu, wd)


def _rope(x, cos, sin_signed, half):
    n, W = x.shape
    lane = lax.broadcasted_iota(I32, (n, LANES), 1)
    first = (lane & 63) < half
    outs = []
    for s in range(W // LANES):
        xs = x[:, s * LANES:(s + 1) * LANES]
        partner = jnp.where(first, pltpu.roll(xs, LANES - half, 1), pltpu.roll(xs, half, 1))
        outs.append(xs * cos + partner * sin_signed)
    return outs[0] if len(outs) == 1 else jnp.concatenate(outs, axis=1)


def _prep_body(x_ref, pre_ref, watt_ref, wsm_ref, wgdn_ref, wz_ref, qan_ref, wqb_ref, wqib_ref,
               lng_ref, lnb_ref, conv_ref, alog_ref, dtb_ref, cosa_ref, sina_ref, cosi_ref, sini_ref,
               q_o, k_o, vt_o, qi_o, ki_o, w_o, gq_o, gk_o, gv_o, z_o, beta_o, g_o,
               pbuf, *, tm):
    tile = pl.program_id(1)
    h = _rms(x_ref[0], pre_ref[...]).astype(BF16)
    cosa, sina = cosa_ref[...], sina_ref[...]
    cosi, sini = cosi_ref[...], sini_ref[...]

    pa = _dot(h, watt_ref[...])
    cq = _rms(pa[:, :Q_LORA_RANK], qan_ref[...]).astype(BF16)
    q = _rope(_dot(cq, wqb_ref[...]), cosa, sina, ATT_HEAD_DIM // 2)
    q_o[0] = (q * (ATT_HEAD_DIM ** -0.5)).astype(BF16)
    qi_o[0] = _rope(_dot(cq, wqib_ref[...]), cosi, sini, IDX_ROPE_DIM // 2).astype(BF16)
    k_o[0] = _rope(pa[:, 256:768], cosa, sina, ATT_HEAD_DIM // 2).astype(BF16)
    vt_o[0, 0] = pa[:, 768:1280].T.astype(BF16)

    ki = pa[:, 1280:1408]
    lane = lax.broadcasted_iota(I32, (tm, LANES), 1)
    real = lane < IDX_DIM
    mu = jnp.sum(jnp.where(real, ki, 0.0), axis=-1, keepdims=True) * (1.0 / IDX_DIM)
    xc = jnp.where(real, ki - mu, 0.0)
    var = jnp.sum(xc * xc, axis=-1, keepdims=True) * (1.0 / IDX_DIM)
    kn = xc * lax.rsqrt(var + NORM_EPS) * lng_ref[...] + lnb_ref[...]
    kr = _rope(kn, cosi, sini, IDX_ROPE_DIM // 2)
    ki_o[0] = jnp.where(real, kr, pltpu.roll(kr, IDX_DIM, 1)).astype(BF16)

    ps = _dot(h, wsm_ref[...])
    w_o[0] = ps[:, :LANES]
    beta_o[0] = jax.nn.sigmoid(ps[:, LANES:2 * LANES])
    xa = ps[:, 2 * LANES:] + dtb_ref[...]
    softplus = jnp.maximum(xa, 0.0) + jnp.log(1.0 + jnp.exp(-jnp.abs(xa)))
    g_o[0] = -jnp.exp(alog_ref[...]) * softplus

    pg = _dot(h, wgdn_ref[...])

    @pl.when(tile == 0)
    def _():
        pbuf[0:8, :] = jnp.zeros((8, pbuf.shape[1]), F32)

    @pl.when(tile > 0)
    def _():
        pbuf[0:8, :] = pbuf[tm:tm + 8, :]

    pbuf[8:tm + 8, :] = pg
    conv = pbuf[5:5 + tm, :] * conv_ref[0:1, :]
    conv = conv + pbuf[6:6 + tm, :] * conv_ref[1:2, :]
    conv = conv + pbuf[7:7 + tm, :] * conv_ref[2:3, :]
    conv = conv + pg * conv_ref[3:4, :]
    act = conv * jax.nn.sigmoid(conv)
    W = GDN_HEADS * GDN_HEAD_DIM
    for hd in range(GDN_HEADS):
        sl = slice(hd * GDN_HEAD_DIM, (hd + 1) * GDN_HEAD_DIM)
        xq = act[:, hd * GDN_HEAD_DIM:(hd + 1) * GDN_HEAD_DIM]
        xk = act[:, W + hd * GDN_HEAD_DIM:W + (hd + 1) * GDN_HEAD_DIM]
        qn = xq * lax.rsqrt(jnp.sum(xq * xq, axis=-1, keepdims=True) + NORM_EPS)
        gq_o[0, :, sl] = qn * (GDN_HEAD_DIM ** -0.5)
        gk_o[0, :, sl] = xk * lax.rsqrt(jnp.sum(xk * xk, axis=-1, keepdims=True) + NORM_EPS)
    gv_o[0] = act[:, 2 * W:]
    z_o[0] = _dot(h, wz_ref[...])


def _prep(x, pre_g, watt, wsm, wgdn, wz, qan, wqb, wqib, lng, lnb, convw, alog, dtb,
          cosa, sina, cosi, sini, *, tm):
    B, L, D = x.shape
    n = L // tm
    full = lambda a: pl.BlockSpec(a.shape, lambda b, i: (0, 0))
    tab = pl.BlockSpec((tm, LANES), lambda b, i: (i, 0))
    tok = lambda w: pl.BlockSpec((1, tm, w), lambda b, i: (b, i, 0))
    vt_spec = pl.BlockSpec((1, 1, 512, tm), lambda b, i: (b, i, 0, 0))
    outs = [(512, BF16), (512, BF16), (512, BF16), (256, BF16), (LANES, BF16), (LANES, F32),
            (512, F32), (512, F32), (512, F32), (512, F32), (LANES, F32), (LANES, F32)]
    return pl.pallas_call(
        functools.partial(_prep_body, tm=tm),
        grid=(B, n),
        in_specs=[tok(D), full(pre_g), full(watt), full(wsm), full(wgdn), full(wz), full(qan),
                  full(wqb), full(wqib), full(lng), full(lnb), full(convw), full(alog), full(dtb),
                  tab, tab, tab, tab],
        out_specs=[vt_spec if idx == 2 else tok(w) for idx, (w, _) in enumerate(outs)],
        out_shape=[jax.ShapeDtypeStruct((B, n, 512, tm) if idx == 2 else (B, L, w), dt)
                   for idx, (w, dt) in enumerate(outs)],
        scratch_shapes=[pltpu.VMEM((tm + 8, wgdn.shape[1]), F32)],
        compiler_params=pltpu.CompilerParams(
            dimension_semantics=("arbitrary", "arbitrary"), vmem_limit_bytes=VMEM_LIMIT),
        name="prep",
    )(x, pre_g, watt, wsm, wgdn, wz, qan, wqb, wqib, lng, lnb, convw, alog, dtb,
      cosa, sina, cosi, sini)


def _fold8(x, op, ways=4):
    parts = [x[r:r + 8] for r in range(0, x.shape[0], 8)]
    accs = parts[:ways]
    for idx, part in enumerate(parts[ways:]):
        accs[idx % ways] = op(accs[idx % ways], part)
    while len(accs) > 1:
        accs = [op(a, b) for a, b in zip(accs[0::2], accs[1::2])] + ([accs[-1]] if len(accs) % 2 else [])
    return accs[0]


def _attn_body(q_ref, qi_ref, w_ref, k_ref, vt_ref, ki_ref, o_ref,
               sc_scr, st_scr, m_scr, l_scr, acc_scr, *, TQ, KC, L, top_k):
    i = pl.program_id(1)
    t0 = i * TQ
    nch = (t0 + TQ + KC - 1) // KC
    qpos = t0 + lax.broadcasted_iota(I32, (1, TQ), 1)
    lane = lax.broadcasted_iota(I32, (TQ, LANES), 1)
    lower = lane < ATT_HEAD_DIM
    zero_bf = jnp.zeros((TQ, LANES), BF16)
    rmc = lax.broadcasted_iota(I32, (KC, TQ), 0) - lax.broadcasted_iota(I32, (KC, TQ), 1)

    def head_halves(ref, n_heads):
        out = []
        for hh in range(n_heads):
            slab = ref[0, :, (hh // 2) * LANES:(hh // 2 + 1) * LANES]
            out.append(jnp.where(lower if hh % 2 == 0 else ~lower, slab, zero_bf))
        return out

    wt = w_ref[0].T
    wrow = [wt[hh:hh + 1, :] for hh in range(IDX_HEADS)]
    qim = head_halves(qi_ref, IDX_HEADS)

    def p1(j, carry):
        lo, hi = carry
        kk = ki_ref[0, pl.ds(pl.multiple_of(j * KC, KC), KC), :]
        sc = jnp.zeros((KC, TQ), F32)
        for hh in range(IDX_HEADS):
            sc = sc + wrow[hh] * jnp.maximum(_dot_nt(kk, qim[hh]), 0.0)
        masked = jnp.where(rmc <= t0 - j * KC, sc, -jnp.inf)
        sc_scr[j] = masked
        lo = jnp.minimum(lo, jnp.min(_fold8(sc, jnp.minimum), axis=0, keepdims=True))
        hi = jnp.maximum(hi, jnp.max(_fold8(masked, jnp.maximum), axis=0, keepdims=True))
        return lo, hi

    lo, top = lax.fori_loop(0, nch, p1, (jnp.full((1, TQ), jnp.inf, F32), jnp.full((1, TQ), -jnp.inf, F32)))

    krow = jnp.minimum(top_k, qpos + 1).astype(F32)

    def count(pred, ways=4):
        def body(j, accs):
            accs = list(accs)
            for r in range(KC // 8):
                hit = pred(sc_scr[j, r * 8:(r + 1) * 8, :], j * KC + r * 8)
                accs[r % ways] = accs[r % ways] + jnp.where(hit, 1.0, 0.0)
            return tuple(accs)
        accs = lax.fori_loop(0, nch, body, tuple(jnp.zeros((8, TQ), F32) for _ in range(ways)))
        return jnp.sum(functools.reduce(jnp.add, accs), axis=0, keepdims=True)

    ncausal = (qpos + 1).astype(F32)
    hi = top + (jnp.abs(top) * (2.0 ** -20) + 1e-30)
    c_ge0 = count(lambda blk, k0: blk >= 0.0)
    c_gt0 = count(lambda blk, k0: blk > 0.0)
    all_rows = ncausal == krow
    at_zero = (c_gt0 < krow) & (c_ge0 >= krow) & ~all_rows
    up = (c_ge0 >= krow) & (lo < 0.0)
    down = (c_ge0 < krow) & (hi > 0.0)
    LO, HI, CLO, CHI, DONE = range(5)
    st_scr[LO:LO + 1, :] = jnp.where(up, 0.0, lo)
    st_scr[CLO:CLO + 1, :] = jnp.where(up, c_ge0, ncausal)
    st_scr[HI:HI + 1, :] = jnp.where(down, 0.0, hi)
    st_scr[CHI:CHI + 1, :] = jnp.where(at_zero, c_gt0, jnp.where(down, c_ge0, 0.0))
    done0 = jnp.where(at_zero | all_rows, 1.0, 0.0)
    st_scr[DONE:DONE + 1, :] = done0

    def bisect(active):
        lo, hi, done = st_scr[LO:LO + 1, :], st_scr[HI:HI + 1, :], st_scr[DONE:DONE + 1, :]
        mid = 0.5 * lo + 0.5 * hi
        stuck = (mid <= lo) | (mid >= hi)
        c = count(lambda blk, k0: blk >= mid)
        ge = c >= krow
        upd = (done == 0.0) & ~stuck
        st_scr[LO:LO + 1, :] = jnp.where(upd & ge, mid, lo)
        st_scr[CLO:CLO + 1, :] = jnp.where(upd & ge, c, st_scr[CLO:CLO + 1, :])
        st_scr[HI:HI + 1, :] = jnp.where(upd & ~ge, mid, hi)
        st_scr[CHI:CHI + 1, :] = jnp.where(upd & ~ge, c, st_scr[CHI:CHI + 1, :])
        new_done = jnp.where(stuck | (upd & (c == krow)), 1.0, done)
        st_scr[DONE:DONE + 1, :] = new_done
        return jnp.max(jnp.where(new_done == 0.0, 1.0, 0.0)).astype(I32)

    lax.while_loop(lambda a: a > 0, bisect, jnp.max(jnp.where(done0 == 0.0, 1.0, 0.0)).astype(I32))

    thr = st_scr[LO:LO + 1, :]
    tie = st_scr[CLO:CLO + 1, :] > krow
    need = krow - st_scr[CHI:CHI + 1, :]

    @pl.when(jnp.max(jnp.where(tie, 1.0, 0.0)) > 0.0)
    def _():
        kidx = lax.broadcasted_iota(I32, (KC, TQ), 0)
        kidx8 = lax.broadcasted_iota(I32, (8, TQ), 0)

        def step(_, carry):
            plo, phi = carry
            pm = (plo + phi) >> 1
            ok = count(lambda blk, k0: (blk == thr) & (kidx8 + k0 <= pm)) >= need
            return jnp.where(ok, plo, pm), jnp.where(ok, pm, phi)

        nbits = int(math.ceil(math.log2(L))) + 1
        _, cut = lax.fori_loop(0, nbits, step,
                               (jnp.full((1, TQ), -1, I32), jnp.full((1, TQ), L - 1, I32)))
        cut = jnp.where(tie, cut, L)

        def drop(j, carry):
            blk = sc_scr[j]
            sc_scr[j] = jnp.where((blk == thr) & (kidx + j * KC > cut), -jnp.inf, blk)
            return carry

        lax.fori_loop(0, nch, drop, 0)

    m_scr[...] = jnp.full(m_scr.shape, NEG_BIG, F32)
    l_scr[...] = jnp.zeros(l_scr.shape, F32)
    acc_scr[...] = jnp.zeros(acc_scr.shape, F32)
    qm = head_halves(q_ref, ATT_HEADS)
    HD = ATT_HEAD_DIM

    heads = range(ATT_HEADS)
    ones_rows = jnp.ones((16, KC), BF16)

    def p3(j, carry):
        start = pl.multiple_of(j * KC, KC)
        bias = jnp.where(sc_scr[j] >= thr, 0.0, NEG_BIG)
        m_old = [m_scr[hh:hh + 1, :] for hh in heads]
        l_old = [l_scr[hh:hh + 1, :] for hh in heads]
        acc_old = [acc_scr[hh] for hh in heads]
        kp = [k_ref[0, pl.ds(start, KC), pr * LANES:(pr + 1) * LANES] for pr in range(ATT_HEADS // 2)]
        vt = [jnp.concatenate([vt_ref[0, j, hh * HD:(hh + 1) * HD, :], ones_rows], axis=0) for hh in heads]
        s = [_dot_nt(kp[hh // 2], qm[hh]) + bias for hh in heads]
        m_new = [jnp.maximum(m_old[hh], jnp.max(_fold8(s[hh], jnp.maximum), axis=0, keepdims=True))
                 for hh in heads]
        alpha = [jnp.exp(m_old[hh] - m_new[hh]) for hh in heads]
        p = [jnp.exp((s[hh] - m_new[hh]).astype(BF16)) for hh in heads]
        pv = [_dot(vt[hh], p[hh]) for hh in heads]
        for hh in heads:
            m_scr[hh:hh + 1, :] = m_new[hh]
            l_scr[hh:hh + 1, :] = alpha[hh] * l_old[hh] + pv[hh][HD:HD + 1, :]
            acc_scr[hh] = alpha[hh] * acc_old[hh] + pv[hh][:HD, :]
        return carry

    lax.fori_loop(0, nch, p3, 0)
    for pr in range(ATT_HEADS // 2):
        halves = [acc_scr[hh] / l_scr[hh:hh + 1, :] for hh in (2 * pr, 2 * pr + 1)]
        o_ref[0, :, pr * LANES:(pr + 1) * LANES] = jnp.concatenate(halves, axis=0).T.astype(o_ref.dtype)


def _attention(q, qi, w, k, vt, ki, *, TQ, KC):
    B, L, _ = q.shape
    top_k = min(TOPK_MAX, L // 4)
    nq = L // TQ
    blk = lambda width: pl.BlockSpec((1, TQ, width), lambda b, i: (b, i, 0))
    whole = lambda width: pl.BlockSpec((1, L, width), lambda b, i: (b, 0, 0),
                                       pipeline_mode=pl.Buffered(1))
    vt_spec = pl.BlockSpec((1, L // KC, 512, KC), lambda b, i: (b, 0, 0, 0), pipeline_mode=pl.Buffered(1))
    return pl.pallas_call(
        functools.partial(_attn_body, TQ=TQ, KC=KC, L=L, top_k=top_k),
        grid=(B, nq),
        in_specs=[blk(512), blk(256), blk(LANES), whole(512), vt_spec, whole(LANES)],
        out_specs=blk(512),
        out_shape=jax.ShapeDtypeStruct((B, L, 512), BF16),
        scratch_shapes=[
            pltpu.VMEM((L // KC, KC, TQ), F32),
            pltpu.VMEM((8, TQ), F32),
            pltpu.VMEM((ATT_HEADS, TQ), F32), pltpu.VMEM((ATT_HEADS, TQ), F32),
            pltpu.VMEM((ATT_HEADS, ATT_HEAD_DIM, TQ), F32),
        ],
        compiler_params=pltpu.CompilerParams(
            dimension_semantics=("arbitrary", "arbitrary"), vmem_limit_bytes=VMEM_LIMIT),
        name="dsa_attention",
    )(q, qi, w, k, vt, ki)


def _mm(a, b):
    return jnp.dot(a.astype(BF16), b.astype(BF16), preferred_element_type=F32)


def _gdn_body(q_ref, k_ref, v_ref, z_ref, g_ref, b_ref, gn_ref, o_ref, s_scr, *, nb):
    C = GDN_CHUNK

    @pl.when(pl.program_id(0) == 0)
    def _():
        s_scr[...] = jnp.zeros_like(s_scr)

    ri = lax.broadcasted_iota(I32, (C, C), 0)
    ci = lax.broadcasted_iota(I32, (C, C), 1)
    incl = ci <= ri
    strict = ci < ri
    eye = ci == ri
    lane = lax.broadcasted_iota(I32, (C, LANES), 1)
    chains = [(b, hd) for b in range(nb) for hd in range(GDN_HEADS)]
    n = range(len(chains))
    sls = [slice(hd * GDN_HEAD_DIM, (hd + 1) * GDN_HEAD_DIM) for _, hd in chains]
    q = [q_ref[b, :, sl] for (b, _), sl in zip(chains, sls)]
    k = [k_ref[b, :, sl] for (b, _), sl in zip(chains, sls)]
    v = [v_ref[b, :, sl] for (b, _), sl in zip(chains, sls)]
    g_col = [jnp.sum(jnp.where(lane == hd, g_ref[b], 0.0), axis=1, keepdims=True) for b, hd in chains]
    beta = [jnp.sum(jnp.where(lane == hd, b_ref[b], 0.0), axis=1, keepdims=True) for b, hd in chains]
    g_row = [jnp.sum(jnp.where(eye, jnp.broadcast_to(x, (C, C)), 0.0), axis=0, keepdims=True) for x in g_col]
    gc_col = [jnp.sum(jnp.where(incl, jnp.broadcast_to(x, (C, C)), 0.0), axis=1, keepdims=True) for x in g_row]
    gc_row = [jnp.sum(jnp.where(ri <= ci, jnp.broadcast_to(x, (C, C)), 0.0), axis=0, keepdims=True) for x in g_col]
    decay = [jnp.where(incl, jnp.exp(jnp.where(incl, gc_col[c] - gc_row[c], 0.0)), 0.0) for c in n]
    eg = [jnp.exp(x) for x in gc_col]
    g_last = [x[C - 1:C, :] for x in gc_col]
    kb = [k[c] * beta[c] for c in n]
    k_bf = [x.astype(BF16) for x in k]
    a_mat = [jnp.where(strict, _dot_nt(kb[c].astype(BF16), k_bf[c]) * decay[c], 0.0) for c in n]
    qk = [jnp.where(incl, _dot_nt(q[c].astype(BF16), k_bf[c]) * decay[c], 0.0) for c in n]
    t_inv = [jnp.where(eye, 1.0, 0.0) - a for a in a_mat]
    pw = a_mat
    for _ in range(5):
        pw = [_mm(x, x) for x in pw]
        t_inv = [t_inv[c] + _mm(t_inv[c], pw[c]) for c in n]
    u = [_mm(t_inv[c], v[c] * beta[c]) for c in n]
    w = [_mm(t_inv[c], kb[c] * eg[c]) for c in n]
    s_old = [s_scr[c] for c in n]
    v_new = [u[c] - _mm(w[c], s_old[c]) for c in n]
    qs = [_mm(q[c] * eg[c], s_old[c]) for c in n]
    for c in n:
        k_tail = k[c] * jnp.exp(g_last[c] - gc_col[c])
        s_scr[c] = s_old[c] * jnp.exp(g_last[c]) + _mm(k_tail.T, v_new[c])
    for c, ((b, _), sl) in enumerate(zip(chains, sls)):
        o = qs[c] + _mm(qk[c], v_new[c])
        z = z_ref[b, :, sl]
        o_ref[b, :, sl] = (_rms(o, gn_ref[...]) * (z * jax.nn.sigmoid(z))).astype(o_ref.dtype)


def _gdn(gq, gk, gv, z, g, beta, gn):
    B, L, W = gq.shape
    C = GDN_CHUNK
    tok = lambda width: pl.BlockSpec((B, C, width), lambda n: (0, n, 0))
    return pl.pallas_call(
        functools.partial(_gdn_body, nb=B),
        grid=(L // C,),
        in_specs=[tok(W), tok(W), tok(W), tok(W), tok(LANES), tok(LANES),
                  pl.BlockSpec(gn.shape, lambda n: (0, 0))],
        out_specs=tok(W),
        out_shape=jax.ShapeDtypeStruct((B, L, W), BF16),
        scratch_shapes=[pltpu.VMEM((B * GDN_HEADS, GDN_HEAD_DIM, GDN_HEAD_DIM), F32)],
        compiler_params=pltpu.CompilerParams(
            dimension_semantics=("arbitrary",), vmem_limit_bytes=VMEM_LIMIT),
        name="gdn",
    )(gq, gk, gv, z, g, beta, gn)


def _out_body(x_ref, a_ref, d_ref, wa_ref, wd_ref, g_ref, o_ref):
    m = _dot(a_ref[...], wa_ref[...]) + _dot(d_ref[...], wd_ref[...])
    o_ref[...] = x_ref[...] + _rms(m, g_ref[...])


def _out_proj(x, att, gdn, wa, wd, post_g, *, tm):
    T, D = x.shape
    W = att.shape[1]
    row = lambda width: pl.BlockSpec((tm, width), lambda i: (i, 0))
    full = lambda a: pl.BlockSpec(a.shape, lambda i: (0, 0))
    return pl.pallas_call(
        _out_body,
        grid=(T // tm,),
        in_specs=[row(D), row(W), row(W), full(wa), full(wd), full(post_g)],
        out_specs=row(D),
        out_shape=jax.ShapeDtypeStruct((T, D), F32),
        compiler_params=pltpu.CompilerParams(
            dimension_semantics=("parallel",), vmem_limit_bytes=VMEM_LIMIT),
        name="out_proj",
    )(x, att, gdn, wa, wd, post_g)


def _rope_tables(L, rot_dim):
    half = rot_dim // 2
    inv_freq = 1.0 / (ROPE_THETA ** (jnp.arange(half, dtype=F32) / half))
    ang = jnp.arange(L, dtype=jnp.int32).astype(F32)[:, None] * inv_freq[None, :]
    cos, sin = jnp.cos(ang), jnp.sin(ang)
    pad = 64 - rot_dim
    cos64 = jnp.concatenate([cos, cos, jnp.ones((L, pad), F32)], axis=1)
    sin64 = jnp.concatenate([-sin, sin, jnp.zeros((L, pad), F32)], axis=1)
    return jnp.tile(cos64, (1, 2)), jnp.tile(sin64, (1, 2))


def _pad_cols(a, width):
    return jnp.pad(a, ((0, 0), (0, width - a.shape[1])))


def _layer(x, p, *, tm_ffn, tf, tm_out, TQ, KC):
    B, L, D = x.shape
    T = B * L
    row = lambda a: a.reshape(1, -1)
    x2 = _ffn(x.reshape(T, D), row(p['ffn1_pre_g']), row(p['ffn1_post_g']),
              p['ffn1_w_gate'].astype(BF16), p['ffn1_w_up'].astype(BF16), p['ffn1_w_down'].astype(BF16),
              tm=tm_ffn, tf=tf)

    w_in = p['w_in']
    sizes = (Q_LORA_RANK, 512, 512, IDX_DIM, IDX_HEADS, 512, 512, 512, 512, GDN_HEADS, GDN_HEADS)
    offs = [0]
    for s in sizes:
        offs.append(offs[-1] + s)
    col = lambda a, b: w_in[:, offs[a]:offs[b]]
    watt = jnp.concatenate([col(0, 3), _pad_cols(col(3, 4), LANES)], axis=1).astype(BF16)
    wsm = jnp.concatenate([_pad_cols(col(4, 5), LANES), _pad_cols(col(9, 10), LANES),
                           _pad_cols(col(10, 11), LANES)], axis=1).astype(BF16)
    wgdn = col(5, 8).astype(BF16)
    wz = col(8, 9).astype(BF16)
    cosa, sina = _rope_tables(L, ATT_HEAD_DIM)
    cosi, sini = _rope_tables(L, IDX_ROPE_DIM)
    q, k, vt, qi, ki, w, gq, gk, gv, z, beta, g = _prep(
        x2.reshape(B, L, D), row(p['mix_pre_g']), watt, wsm, wgdn, wz, row(p['q_a_norm_g']),
        p['w_q_b'].astype(BF16), p['w_qidx_b'].astype(BF16),
        _pad_cols(row(p['kidx_ln_g']), LANES), _pad_cols(row(p['kidx_ln_b']), LANES),
        p['gdn_conv_w'], _pad_cols(row(p['gdn_a_log']), LANES), _pad_cols(row(p['gdn_dt_bias']), LANES),
        cosa, sina, cosi, sini, tm=KC)

    att = _attention(q, qi, w, k, vt, ki, TQ=TQ, KC=KC)
    gdn = _gdn(gq, gk, gv, z, g, beta, row(p['gdn_out_norm_g']))
    w_out = p['w_out'].astype(BF16)
    x3 = _out_proj(x2, att.reshape(T, -1), gdn.reshape(T, -1), w_out[:512], w_out[512:],
                   row(p['mix_post_g']), tm=tm_out)
    x4 = _ffn(x3, row(p['ffn2_pre_g']), row(p['ffn2_post_g']),
              p['ffn2_w_gate'].astype(BF16), p['ffn2_w_up'].astype(BF16), p['ffn2_w_down'].astype(BF16),
              tm=tm_ffn, tf=tf)
    return x4.reshape(B, L, D)


_NAMES = ('ffn1_pre_g', 'ffn1_post_g', 'ffn1_w_gate', 'ffn1_w_up', 'ffn1_w_down', 'mix_pre_g',
          'mix_post_g', 'w_in', 'q_a_norm_g', 'w_q_b', 'w_qidx_b', 'kidx_ln_g', 'kidx_ln_b',
          'gdn_conv_w', 'gdn_a_log', 'gdn_dt_bias', 'gdn_out_norm_g', 'w_out', 'ffn2_pre_g',
          'ffn2_post_g', 'ffn2_w_gate', 'ffn2_w_up', 'ffn2_w_down')


def kernel(x, ffn1_pre_g, ffn1_post_g, ffn1_w_gate, ffn1_w_up, ffn1_w_down, mix_pre_g, mix_post_g, w_in, q_a_norm_g, w_q_b, w_qidx_b, kidx_ln_g, kidx_ln_b, gdn_conv_w, gdn_a_log, gdn_dt_bias, gdn_out_norm_g, w_out, ffn2_pre_g, ffn2_post_g, ffn2_w_gate, ffn2_w_up, ffn2_w_down):
    args = (ffn1_pre_g, ffn1_post_g, ffn1_w_gate, ffn1_w_up, ffn1_w_down, mix_pre_g, mix_post_g,
            w_in, q_a_norm_g, w_q_b, w_qidx_b, kidx_ln_g, kidx_ln_b, gdn_conv_w, gdn_a_log,
            gdn_dt_bias, gdn_out_norm_g, w_out, ffn2_pre_g, ffn2_post_g, ffn2_w_gate, ffn2_w_up,
            ffn2_w_down)
    for layer in range(ffn1_pre_g.shape[0]):
        p = {n: a[layer] for n, a in zip(_NAMES, args)}
        L = x.shape[1]
        x = _layer(x, p, tm_ffn=min(1024, L), tf=1408, tm_out=min(1024, L),
                   TQ=min(256, L), KC=min(512, L))
    return x
```

```python
import functools
import math

import jax
import jax.numpy as jnp
from jax import lax
from jax.experimental import pallas as pl
from jax.experimental.pallas import tpu as pltpu

F32 = jnp.float32
BF16 = jnp.bfloat16
I32 = jnp.int32

LANES = 128
ATT_HEAD_DIM = 64
ATT_HEADS = 8
IDX_HEADS = 4
IDX_DIM = 64
IDX_ROPE_DIM = 32
Q_LORA_RANK = 256
TOPK_MAX = 256
GDN_HEAD_DIM = 128
GDN_HEADS = 4
GDN_CONV = 4
GDN_CHUNK = 64
ROPE_THETA = 10000.0
NORM_EPS = 1e-6
NEG_BIG = -1e30

VMEM_LIMIT = 56 * 1024 * 1024


def _rms(x, g):
    return x * lax.rsqrt(jnp.mean(x * x, axis=-1, keepdims=True) + NORM_EPS) * g


def _dot(a, b):
    return jnp.dot(a, b, preferred_element_type=F32)


def _dot_nt(a, b):
    return lax.dot_general(a, b, (((1,), (1,)), ((), ())), preferred_element_type=F32)


def _ffn_body(x_ref, pre_ref, post_ref, wg_ref, wu_ref, wd_ref, o_ref, xn_ref, acc_ref):
    j = pl.program_id(1)

    @pl.when(j == 0)
    def _():
        xn_ref[...] = _rms(x_ref[...], pre_ref[...]).astype(BF16)
        acc_ref[...] = jnp.zeros_like(acc_ref)

    xn = xn_ref[...]
    g = _dot(xn, wg_ref[...])
    u = _dot(xn, wu_ref[...])
    h = (g * jax.nn.sigmoid(g) * u).astype(BF16)
    acc_ref[...] += _dot(h, wd_ref[...])

    @pl.when(j == pl.num_programs(1) - 1)
    def _():
        o_ref[...] = x_ref[...] + 0.5 * _rms(acc_ref[...], post_ref[...])


def _ffn(x, pre_g, post_g, wg, wu, wd, *, tm, tf):
    T, D = x.shape
    F = wg.shape[1]
    return pl.pallas_call(
        _ffn_body,
        grid=(T // tm, F // tf),
        in_specs=[
            pl.BlockSpec((tm, D), lambda i, j: (i, 0)),
            pl.BlockSpec((1, D), lambda i, j: (0, 0)),
            pl.BlockSpec((1, D), lambda i, j: (0, 0)),
            pl.BlockSpec((D, tf), lambda i, j: (0, j)),
            pl.BlockSpec((D, tf), lambda i, j: (0, j)),
            pl.BlockSpec((tf, D), lambda i, j: (j, 0)),
        ],
        out_specs=pl.BlockSpec((tm, D), lambda i, j: (i, 0)),
        out_shape=jax.ShapeDtypeStruct((T, D), F32),
        scratch_shapes=[pltpu.VMEM((tm, D), BF16), pltpu.VMEM((tm, D), F32)],
        compiler_params=pltpu.CompilerParams(
            dimension_semantics=("parallel", "arbitrary"), vmem_limit_bytes=VMEM_LIMIT),
        name="ffn",
    )(x, pre_g, post_g, wg, wu, wd)


def _rope(x, cos, sin_signed, half):
    n, W = x.shape
    lane = lax.broadcasted_iota(I32, (n, LANES), 1)
    first = (lane & 63) < half
    outs = []
    for s in range(W // LANES):
        xs = x[:, s * LANES:(s + 1) * LANES]
        partner = jnp.where(first, pltpu.roll(xs, LANES - half, 1), pltpu.roll(xs, half, 1))
        outs.append(xs * cos + partner * sin_signed)
    return outs[0] if len(outs) == 1 else jnp.concatenate(outs, axis=1)


def _prep_body(x_ref, pre_ref, watt_ref, wsm_ref, wgdn_ref, wz_ref, qan_ref, wqb_ref, wqib_ref,
               lng_ref, lnb_ref, conv_ref, alog_ref, dtb_ref, cosa_ref, sina_ref, cosi_ref, sini_ref,
               q_o, k_o, vt_o, qi_o, ki_o, w_o, gq_o, gk_o, gv_o, z_o, beta_o, g_o,
               pbuf, *, tm):
    tile = pl.program_id(1)
    h = _rms(x_ref[0], pre_ref[...]).astype(BF16)
    cosa, sina = cosa_ref[...], sina_ref[...]
    cosi, sini = cosi_ref[...], sini_ref[...]

    pa = _dot(h, watt_ref[...])
    cq = _rms(pa[:, :Q_LORA_RANK], qan_ref[...]).astype(BF16)
    q = _rope(_dot(cq, wqb_ref[...]), cosa, sina, ATT_HEAD_DIM // 2)
    q_o[0] = (q * (ATT_HEAD_DIM ** -0.5)).astype(BF16)
    qi_o[0] = _rope(_dot(cq, wqib_ref[...]), cosi, sini, IDX_ROPE_DIM // 2).astype(BF16)
    k_o[0] = _rope(pa[:, 256:768], cosa, sina, ATT_HEAD_DIM // 2).astype(BF16)
    vt_o[0, 0] = pa[:, 768:1280].T.astype(BF16)

    ki = pa[:, 1280:1408]
    lane = lax.broadcasted_iota(I32, (tm, LANES), 1)
    real = lane < IDX_DIM
    mu = jnp.sum(jnp.where(real, ki, 0.0), axis=-1, keepdims=True) * (1.0 / IDX_DIM)
    xc = jnp.where(real, ki - mu, 0.0)
    var = jnp.sum(xc * xc, axis=-1, keepdims=True) * (1.0 / IDX_DIM)
    kn = xc * lax.rsqrt(var + NORM_EPS) * lng_ref[...] + lnb_ref[...]
    kr = _rope(kn, cosi, sini, IDX_ROPE_DIM // 2)
    ki_o[0] = jnp.where(real, kr, pltpu.roll(kr, IDX_DIM, 1)).astype(BF16)

    ps = _dot(h, wsm_ref[...])
    w_o[0] = ps[:, :LANES]
    beta_o[0] = jax.nn.sigmoid(ps[:, LANES:2 * LANES])
    xa = ps[:, 2 * LANES:] + dtb_ref[...]
    softplus = jnp.maximum(xa, 0.0) + jnp.log(1.0 + jnp.exp(-jnp.abs(xa)))
    g_o[0] = -jnp.exp(alog_ref[...]) * softplus

    pg = _dot(h, wgdn_ref[...])

    @pl.when(tile == 0)
    def _():
        pbuf[0:8, :] = jnp.zeros((8, pbuf.shape[1]), F32)

    @pl.when(tile > 0)
    def _():
        pbuf[0:8, :] = pbuf[tm:tm + 8, :]

    pbuf[8:tm + 8, :] = pg
    conv = pbuf[5:5 + tm, :] * conv_ref[0:1, :]
    conv = conv + pbuf[6:6 + tm, :] * conv_ref[1:2, :]
    conv = conv + pbuf[7:7 + tm, :] * conv_ref[2:3, :]
    conv = conv + pg * conv_ref[3:4, :]
    act = conv * jax.nn.sigmoid(conv)
    W = GDN_HEADS * GDN_HEAD_DIM
    for hd in range(GDN_HEADS):
        sl = slice(hd * GDN_HEAD_DIM, (hd + 1) * GDN_HEAD_DIM)
        xq = act[:, hd * GDN_HEAD_DIM:(hd + 1) * GDN_HEAD_DIM]
        xk = act[:, W + hd * GDN_HEAD_DIM:W + (hd + 1) * GDN_HEAD_DIM]
        qn = xq * lax.rsqrt(jnp.sum(xq * xq, axis=-1, keepdims=True) + NORM_EPS)
        gq_o[0, :, sl] = qn * (GDN_HEAD_DIM ** -0.5)
        gk_o[0, :, sl] = xk * lax.rsqrt(jnp.sum(xk * xk, axis=-1, keepdims=True) + NORM_EPS)
    gv_o[0] = act[:, 2 * W:]
    z_o[0] = _dot(h, wz_ref[...])


def _prep(x, pre_g, watt, wsm, wgdn, wz, qan, wqb, wqib, lng, lnb, convw, alog, dtb,
          cosa, sina, cosi, sini, *, tm):
    B, L, D = x.shape
    n = L // tm
    full = lambda a: pl.BlockSpec(a.shape, lambda b, i: (0, 0))
    tab = pl.BlockSpec((tm, LANES), lambda b, i: (i, 0))
    tok = lambda w: pl.BlockSpec((1, tm, w), lambda b, i: (b, i, 0))
    vt_spec = pl.BlockSpec((1, 1, 512, tm), lambda b, i: (b, i, 0, 0))
    outs = [(512, BF16), (512, BF16), (512, BF16), (256, BF16), (LANES, BF16), (LANES, F32),
            (512, F32), (512, F32), (512, F32), (512, F32), (LANES, F32), (LANES, F32)]
    return pl.pallas_call(
        functools.partial(_prep_body, tm=tm),
        grid=(B, n),
        in_specs=[tok(D), full(pre_g), full(watt), full(wsm), full(wgdn), full(wz), full(qan),
                  full(wqb), full(wqib), full(lng), full(lnb), full(convw), full(alog), full(dtb),
                  tab, tab, tab, tab],
        out_specs=[vt_spec if idx == 2 else tok(w) for idx, (w, _) in enumerate(outs)],
        out_shape=[jax.ShapeDtypeStruct((B, n, 512, tm) if idx == 2 else (B, L, w), dt)
                   for idx, (w, dt) in enumerate(outs)],
        scratch_shapes=[pltpu.VMEM((tm + 8, wgdn.shape[1]), F32)],
        compiler_params=pltpu.CompilerParams(
            dimension_semantics=("arbitrary", "arbitrary"), vmem_limit_bytes=VMEM_LIMIT),
        name="prep",
    )(x, pre_g, watt, wsm, wgdn, wz, qan, wqb, wqib, lng, lnb, convw, alog, dtb,
      cosa, sina, cosi, sini)


def _fold8(x, op, ways=4):
    parts = [x[r:r + 8] for r in range(0, x.shape[0], 8)]
    accs = parts[:ways]
    for idx, part in enumerate(parts[ways:]):
        accs[idx % ways] = op(accs[idx % ways], part)
    while len(accs) > 1:
        accs = [op(a, b) for a, b in zip(accs[0::2], accs[1::2])] + ([accs[-1]] if len(accs) % 2 else [])
    return accs[0]


def _attn_body(q_ref, qi_ref, w_ref, k_ref, vt_ref, ki_ref, o_ref,
               sc_scr, st_scr, m_scr, l_scr, acc_scr, *, TQ, KC, L, top_k):
    i = pl.program_id(1)
    t0 = i * TQ
    nch = (t0 + TQ + KC - 1) // KC
    qpos = t0 + lax.broadcasted_iota(I32, (1, TQ), 1)
    lane = lax.broadcasted_iota(I32, (TQ, LANES), 1)
    lower = lane < ATT_HEAD_DIM
    zero_bf = jnp.zeros((TQ, LANES), BF16)
    rmc = lax.broadcasted_iota(I32, (KC, TQ), 0) - lax.broadcasted_iota(I32, (KC, TQ), 1)

    def head_halves(ref, n_heads):
        out = []
        for hh in range(n_heads):
            slab = ref[0, :, (hh // 2) * LANES:(hh // 2 + 1) * LANES]
            out.append(jnp.where(lower if hh % 2 == 0 else ~lower, slab, zero_bf))
        return out

    wt = w_ref[0].T
    wrow = [wt[hh:hh + 1, :] for hh in range(IDX_HEADS)]
    qim = head_halves(qi_ref, IDX_HEADS)

    def p1(j, carry):
        lo, hi = carry
        kk = ki_ref[0, pl.ds(pl.multiple_of(j * KC, KC), KC), :]
        sc = jnp.zeros((KC, TQ), F32)
        for hh in range(IDX_HEADS):
            sc = sc + wrow[hh] * jnp.maximum(_dot_nt(kk, qim[hh]), 0.0)
        masked = jnp.where(rmc <= t0 - j * KC, sc, -jnp.inf)
        sc_scr[j] = masked
        lo = jnp.minimum(lo, jnp.min(_fold8(sc, jnp.minimum), axis=0, keepdims=True))
        hi = jnp.maximum(hi, jnp.max(_fold8(masked, jnp.maximum), axis=0, keepdims=True))
        return lo, hi

    lo, top = lax.fori_loop(0, nch, p1, (jnp.full((1, TQ), jnp.inf, F32), jnp.full((1, TQ), -jnp.inf, F32)))

    krow = jnp.minimum(top_k, qpos + 1).astype(F32)

    def count(pred, ways=4):
        def body(j, accs):
            accs = list(accs)
            for r in range(KC // 8):
                hit = pred(sc_scr[j, r * 8:(r + 1) * 8, :], j * KC + r * 8)
                accs[r % ways] = accs[r % ways] + jnp.where(hit, 1.0, 0.0)
            return tuple(accs)
        accs = lax.fori_loop(0, nch, body, tuple(jnp.zeros((8, TQ), F32) for _ in range(ways)))
        return jnp.sum(functools.reduce(jnp.add, accs), axis=0, keepdims=True)

    ncausal = (qpos + 1).astype(F32)
    hi = top + (jnp.abs(top) * (2.0 ** -20) + 1e-30)
    c_ge0 = count(lambda blk, k0: blk >= 0.0)
    c_gt0 = count(lambda blk, k0: blk > 0.0)
    all_rows = ncausal == krow
    at_zero = (c_gt0 < krow) & (c_ge0 >= krow) & ~all_rows
    up = (c_ge0 >= krow) & (lo < 0.0)
    down = (c_ge0 < krow) & (hi > 0.0)
    LO, HI, CLO, CHI, DONE = range(5)
    st_scr[LO:LO + 1, :] = jnp.where(up, 0.0, lo)
    st_scr[CLO:CLO + 1, :] = jnp.where(up, c_ge0, ncausal)
    st_scr[HI:HI + 1, :] = jnp.where(down, 0.0, hi)
    st_scr[CHI:CHI + 1, :] = jnp.where(at_zero, c_gt0, jnp.where(down, c_ge0, 0.0))
    done0 = jnp.where(at_zero | all_rows, 1.0, 0.0)
    st_scr[DONE:DONE + 1, :] = done0

    def bisect(active):
        lo, hi, done = st_scr[LO:LO + 1, :], st_scr[HI:HI + 1, :], st_scr[DONE:DONE + 1, :]
        mid = 0.5 * lo + 0.5 * hi
        stuck = (mid <= lo) | (mid >= hi)
        c = count(lambda blk, k0: blk >= mid)
        ge = c >= krow
        upd = (done == 0.0) & ~stuck
        st_scr[LO:LO + 1, :] = jnp.where(upd & ge, mid, lo)
        st_scr[CLO:CLO + 1, :] = jnp.where(upd & ge, c, st_scr[CLO:CLO + 1, :])
        st_scr[HI:HI + 1, :] = jnp.where(upd & ~ge, mid, hi)
        st_scr[CHI:CHI + 1, :] = jnp.where(upd & ~ge, c, st_scr[CHI:CHI + 1, :])
        new_done = jnp.where(stuck | (upd & (c == krow)), 1.0, done)
        st_scr[DONE:DONE + 1, :] = new_done
        return jnp.max(jnp.where(new_done == 0.0, 1.0, 0.0)).astype(I32)

    lax.while_loop(lambda a: a > 0, bisect, jnp.max(jnp.where(done0 == 0.0, 1.0, 0.0)).astype(I32))

    thr = st_scr[LO:LO + 1, :]
    tie = st_scr[CLO:CLO + 1, :] > krow
    need = krow - st_scr[CHI:CHI + 1, :]

    @pl.when(jnp.max(jnp.where(tie, 1.0, 0.0)) > 0.0)
    def _():
        below = jnp.where(lax.broadcasted_iota(I32, (KC, KC), 1) < lax.broadcasted_iota(I32, (KC, KC), 0),
                          1.0, 0.0).astype(BF16)

        def drop(j, seen):
            blk = sc_scr[j]
            tied = (blk == thr) & tie
            ind = jnp.where(tied, 1.0, 0.0)
            rank = _dot(below, ind.astype(BF16)) + seen
            sc_scr[j] = jnp.where(tied & (rank >= need), -jnp.inf, blk)
            return rank[KC - 1:KC, :] + ind[KC - 1:KC, :]

        lax.fori_loop(0, nch, drop, jnp.zeros((1, TQ), F32))

    m_scr[...] = jnp.full(m_scr.shape, NEG_BIG, F32)
    l_scr[...] = jnp.zeros(l_scr.shape, F32)
    acc_scr[...] = jnp.zeros(acc_scr.shape, F32)
    qm = head_halves(q_ref, ATT_HEADS)
    HD = ATT_HEAD_DIM

    heads = range(ATT_HEADS)
    ones_rows = jnp.ones((16, KC), BF16)

    def p3(j, carry):
        start = pl.multiple_of(j * KC, KC)
        bias = jnp.where(sc_scr[j] >= thr, 0.0, NEG_BIG)
        m_old = [m_scr[hh:hh + 1, :] for hh in heads]
        l_old = [l_scr[hh:hh + 1, :] for hh in heads]
        acc_old = [acc_scr[hh] for hh in heads]
        kp = [k_ref[0, pl.ds(start, KC), pr * LANES:(pr + 1) * LANES] for pr in range(ATT_HEADS // 2)]
        vt = [jnp.concatenate([vt_ref[0, j, hh * HD:(hh + 1) * HD, :], ones_rows], axis=0) for hh in heads]
        s = [_dot_nt(kp[hh // 2], qm[hh]) + bias for hh in heads]
        m_new = [jnp.maximum(m_old[hh], jnp.max(_fold8(s[hh], jnp.maximum), axis=0, keepdims=True))
                 for hh in heads]
        alpha = [jnp.exp(m_old[hh] - m_new[hh]) for hh in heads]
        p = [jnp.exp((s[hh] - m_new[hh]).astype(BF16)) for hh in heads]
        pv = [_dot(vt[hh], p[hh]) for hh in heads]
        for hh in heads:
            m_scr[hh:hh + 1, :] = m_new[hh]
            l_scr[hh:hh + 1, :] = alpha[hh] * l_old[hh] + pv[hh][HD:HD + 1, :]
            acc_scr[hh] = alpha[hh] * acc_old[hh] + pv[hh][:HD, :]
        return carry

    lax.fori_loop(0, nch, p3, 0)
    for pr in range(ATT_HEADS // 2):
        halves = [acc_scr[hh] / l_scr[hh:hh + 1, :] for hh in (2 * pr, 2 * pr + 1)]
        o_ref[0, :, pr * LANES:(pr + 1) * LANES] = jnp.concatenate(halves, axis=0).T.astype(o_ref.dtype)


def _attention(q, qi, w, k, vt, ki, *, TQ, KC):
    B, L, _ = q.shape
    top_k = min(TOPK_MAX, L // 4)
    nq = L // TQ
    blk = lambda width: pl.BlockSpec((1, TQ, width), lambda b, i: (b, i, 0))
    whole = lambda width: pl.BlockSpec((1, L, width), lambda b, i: (b, 0, 0),
                                       pipeline_mode=pl.Buffered(1))
    vt_spec = pl.BlockSpec((1, L // KC, 512, KC), lambda b, i: (b, 0, 0, 0), pipeline_mode=pl.Buffered(1))
    return pl.pallas_call(
        functools.partial(_attn_body, TQ=TQ, KC=KC, L=L, top_k=top_k),
        grid=(B, nq),
        in_specs=[blk(512), blk(256), blk(LANES), whole(512), vt_spec, whole(LANES)],
        out_specs=blk(512),
        out_shape=jax.ShapeDtypeStruct((B, L, 512), BF16),
        scratch_shapes=[
            pltpu.VMEM((L // KC, KC, TQ), F32),
            pltpu.VMEM((8, TQ), F32),
            pltpu.VMEM((ATT_HEADS, TQ), F32), pltpu.VMEM((ATT_HEADS, TQ), F32),
            pltpu.VMEM((ATT_HEADS, ATT_HEAD_DIM, TQ), F32),
        ],
        compiler_params=pltpu.CompilerParams(
            dimension_semantics=("arbitrary", "arbitrary"), vmem_limit_bytes=VMEM_LIMIT),
        name="dsa_attention",
    )(q, qi, w, k, vt, ki)


def _mm(a, b):
    return jnp.dot(a.astype(BF16), b.astype(BF16), preferred_element_type=F32)


def _gdn_body(q_ref, k_ref, v_ref, z_ref, g_ref, b_ref, gn_ref, o_ref, s_scr, *, nb, G):
    C = GDN_CHUNK

    @pl.when(pl.program_id(0) == 0)
    def _():
        s_scr[...] = jnp.zeros_like(s_scr)

    ri = lax.broadcasted_iota(I32, (C, C), 0)
    ci = lax.broadcasted_iota(I32, (C, C), 1)
    incl = ci <= ri
    strict = ci < ri
    eye = ci == ri
    lane = lax.broadcasted_iota(I32, (C, LANES), 1)
    chains = [(b, hd) for b in range(nb) for hd in range(GDN_HEADS)]
    units = [(gi, b, hd) for gi in range(G) for b, hd in chains]
    n = range(len(units))
    rows = [slice(gi * C, (gi + 1) * C) for gi, _, _ in units]
    cols = [slice(hd * GDN_HEAD_DIM, (hd + 1) * GDN_HEAD_DIM) for _, _, hd in units]
    q = [q_ref[b, rows[i], cols[i]] for i, (_, b, _) in enumerate(units)]
    k = [k_ref[b, rows[i], cols[i]] for i, (_, b, _) in enumerate(units)]
    v = [v_ref[b, rows[i], cols[i]] for i, (_, b, _) in enumerate(units)]
    z = [z_ref[b, rows[i], cols[i]] for i, (_, b, _) in enumerate(units)]
    g_col = [jnp.sum(jnp.where(lane == hd, g_ref[b, rows[i], :], 0.0), axis=1, keepdims=True)
             for i, (_, b, hd) in enumerate(units)]
    beta = [jnp.sum(jnp.where(lane == hd, b_ref[b, rows[i], :], 0.0), axis=1, keepdims=True)
            for i, (_, b, hd) in enumerate(units)]
    s_in = [s_scr[c] for c in range(len(chains))]
    g_row = [jnp.sum(jnp.where(eye, jnp.broadcast_to(x, (C, C)), 0.0), axis=0, keepdims=True) for x in g_col]
    gc_col = [jnp.sum(jnp.where(incl, jnp.broadcast_to(x, (C, C)), 0.0), axis=1, keepdims=True) for x in g_row]
    gc_row = [jnp.sum(jnp.where(ri <= ci, jnp.broadcast_to(x, (C, C)), 0.0), axis=0, keepdims=True) for x in g_col]
    decay = [jnp.where(incl, jnp.exp(jnp.where(incl, gc_col[i] - gc_row[i], 0.0)), 0.0) for i in n]
    eg = [jnp.exp(x) for x in gc_col]
    g_last = [x[C - 1:C, :] for x in gc_col]
    kb = [k[i] * beta[i] for i in n]
    k_bf = [x.astype(BF16) for x in k]
    a_mat = [jnp.where(strict, _dot_nt(kb[i].astype(BF16), k_bf[i]) * decay[i], 0.0) for i in n]
    qk = [jnp.where(incl, _dot_nt(q[i].astype(BF16), k_bf[i]) * decay[i], 0.0) for i in n]
    t_inv = [jnp.where(eye, 1.0, 0.0) - a for a in a_mat]
    pw = a_mat
    for _ in range(5):
        pw = [_mm(x, x) for x in pw]
        t_inv = [t_inv[i] + _mm(t_inv[i], pw[i]) for i in n]
    u = [_mm(t_inv[i], v[i] * beta[i]) for i in n]
    w = [_mm(t_inv[i], kb[i] * eg[i]) for i in n]
    q_dec = [q[i] * eg[i] for i in n]
    k_tail_t = [(k[i] * jnp.exp(g_last[i] - gc_col[i])).T for i in n]
    a_last = [jnp.exp(x) for x in g_last]
    gate = [x * jax.nn.sigmoid(x) for x in z]
    state = s_in
    outs = []
    for gi in range(G):
        ids = [gi * len(chains) + c for c in range(len(chains))]
        v_new = [u[i] - _mm(w[i], state[c]) for c, i in enumerate(ids)]
        qs = [_mm(q_dec[i], state[c]) for c, i in enumerate(ids)]
        state = [state[c] * a_last[i] + _mm(k_tail_t[i], v_new[c]) for c, i in enumerate(ids)]
        outs += [qs[c] + _mm(qk[i], v_new[c]) for c, i in enumerate(ids)]
    for c in range(len(chains)):
        s_scr[c] = state[c]
    for i, (_, b, _) in enumerate(units):
        o_ref[b, rows[i], cols[i]] = (_rms(outs[i], gn_ref[...]) * gate[i]).astype(o_ref.dtype)


def _gdn(gq, gk, gv, z, g, beta, gn, *, G):
    B, L, W = gq.shape
    R = G * GDN_CHUNK
    tok = lambda width: pl.BlockSpec((B, R, width), lambda n: (0, n, 0))
    return pl.pallas_call(
        functools.partial(_gdn_body, nb=B, G=G),
        grid=(L // R,),
        in_specs=[tok(W), tok(W), tok(W), tok(W), tok(LANES), tok(LANES),
                  pl.BlockSpec(gn.shape, lambda n: (0, 0))],
        out_specs=tok(W),
        out_shape=jax.ShapeDtypeStruct((B, L, W), BF16),
        scratch_shapes=[pltpu.VMEM((B * GDN_HEADS, GDN_HEAD_DIM, GDN_HEAD_DIM), F32)],
        compiler_params=pltpu.CompilerParams(
            dimension_semantics=("arbitrary",), vmem_limit_bytes=VMEM_LIMIT),
        name="gdn",
    )(gq, gk, gv, z, g, beta, gn)


def _out_body(x_ref, a_ref, d_ref, wa_ref, wd_ref, g_ref, o_ref):
    m = _dot(a_ref[...], wa_ref[...]) + _dot(d_ref[...], wd_ref[...])
    o_ref[...] = x_ref[...] + _rms(m, g_ref[...])


def _out_proj(x, att, gdn, wa, wd, post_g, *, tm):
    T, D = x.shape
    W = att.shape[1]
    row = lambda width: pl.BlockSpec((tm, width), lambda i: (i, 0))
    full = lambda a: pl.BlockSpec(a.shape, lambda i: (0, 0))
    return pl.pallas_call(
        _out_body,
        grid=(T // tm,),
        in_specs=[row(D), row(W), row(W), full(wa), full(wd), full(post_g)],
        out_specs=row(D),
        out_shape=jax.ShapeDtypeStruct((T, D), F32),
        compiler_params=pltpu.CompilerParams(
            dimension_semantics=("parallel",), vmem_limit_bytes=VMEM_LIMIT),
        name="out_proj",
    )(x, att, gdn, wa, wd, post_g)


def _rope_tables(L, rot_dim):
    half = rot_dim // 2
    inv_freq = 1.0 / (ROPE_THETA ** (jnp.arange(half, dtype=F32) / half))
    ang = jnp.arange(L, dtype=jnp.int32).astype(F32)[:, None] * inv_freq[None, :]
    cos, sin = jnp.cos(ang), jnp.sin(ang)
    pad = 64 - rot_dim
    cos64 = jnp.concatenate([cos, cos, jnp.ones((L, pad), F32)], axis=1)
    sin64 = jnp.concatenate([-sin, sin, jnp.zeros((L, pad), F32)], axis=1)
    return jnp.tile(cos64, (1, 2)), jnp.tile(sin64, (1, 2))


def _pad_cols(a, width):
    return jnp.pad(a, ((0, 0), (0, width - a.shape[1])))


def _layer(x, p, *, tm_ffn, tf, tm_out, TQ, KC):
    B, L, D = x.shape
    T = B * L
    row = lambda a: a.reshape(1, -1)
    x2 = _ffn(x.reshape(T, D), row(p['ffn1_pre_g']), row(p['ffn1_post_g']),
              p['ffn1_w_gate'].astype(BF16), p['ffn1_w_up'].astype(BF16), p['ffn1_w_down'].astype(BF16),
              tm=tm_ffn, tf=tf)

    w_in = p['w_in']
    sizes = (Q_LORA_RANK, 512, 512, IDX_DIM, IDX_HEADS, 512, 512, 512, 512, GDN_HEADS, GDN_HEADS)
    offs = [0]
    for s in sizes:
        offs.append(offs[-1] + s)
    col = lambda a, b: w_in[:, offs[a]:offs[b]]
    watt = jnp.concatenate([col(0, 3), _pad_cols(col(3, 4), LANES)], axis=1).astype(BF16)
    wsm = jnp.concatenate([_pad_cols(col(4, 5), LANES), _pad_cols(col(9, 10), LANES),
                           _pad_cols(col(10, 11), LANES)], axis=1).astype(BF16)
    wgdn = col(5, 8).astype(BF16)
    wz = col(8, 9).astype(BF16)
    cosa, sina = _rope_tables(L, ATT_HEAD_DIM)
    cosi, sini = _rope_tables(L, IDX_ROPE_DIM)
    q, k, vt, qi, ki, w, gq, gk, gv, z, beta, g = _prep(
        x2.reshape(B, L, D), row(p['mix_pre_g']), watt, wsm, wgdn, wz, row(p['q_a_norm_g']),
        p['w_q_b'].astype(BF16), p['w_qidx_b'].astype(BF16),
        _pad_cols(row(p['kidx_ln_g']), LANES), _pad_cols(row(p['kidx_ln_b']), LANES),
        p['gdn_conv_w'], _pad_cols(row(p['gdn_a_log']), LANES), _pad_cols(row(p['gdn_dt_bias']), LANES),
        cosa, sina, cosi, sini, tm=KC)

    att = _attention(q, qi, w, k, vt, ki, TQ=TQ, KC=KC)
    gdn = _gdn(gq, gk, gv, z, g, beta, row(p['gdn_out_norm_g']), G=4)
    w_out = p['w_out'].astype(BF16)
    x3 = _out_proj(x2, att.reshape(T, -1), gdn.reshape(T, -1), w_out[:512], w_out[512:],
                   row(p['mix_post_g']), tm=tm_out)
    x4 = _ffn(x3, row(p['ffn2_pre_g']), row(p['ffn2_post_g']),
              p['ffn2_w_gate'].astype(BF16), p['ffn2_w_up'].astype(BF16), p['ffn2_w_down'].astype(BF16),
              tm=tm_ffn, tf=tf)
    return x4.reshape(B, L, D)


_NAMES = ('ffn1_pre_g', 'ffn1_post_g', 'ffn1_w_gate', 'ffn1_w_up', 'ffn1_w_down', 'mix_pre_g',
          'mix_post_g', 'w_in', 'q_a_norm_g', 'w_q_b', 'w_qidx_b', 'kidx_ln_g', 'kidx_ln_b',
          'gdn_conv_w', 'gdn_a_log', 'gdn_dt_bias', 'gdn_out_norm_g', 'w_out', 'ffn2_pre_g',
          'ffn2_post_g', 'ffn2_w_gate', 'ffn2_w_up', 'ffn2_w_down')


def kernel(x, ffn1_pre_g, ffn1_post_g, ffn1_w_gate, ffn1_w_up, ffn1_w_down, mix_pre_g, mix_post_g, w_in, q_a_norm_g, w_q_b, w_qidx_b, kidx_ln_g, kidx_ln_b, gdn_conv_w, gdn_a_log, gdn_dt_bias, gdn_out_norm_g, w_out, ffn2_pre_g, ffn2_post_g, ffn2_w_gate, ffn2_w_up, ffn2_w_down):
    args = (ffn1_pre_g, ffn1_post_g, ffn1_w_gate, ffn1_w_up, ffn1_w_down, mix_pre_g, mix_post_g,
            w_in, q_a_norm_g, w_q_b, w_qidx_b, kidx_ln_g, kidx_ln_b, gdn_conv_w, gdn_a_log,
            gdn_dt_bias, gdn_out_norm_g, w_out, ffn2_pre_g, ffn2_post_g, ffn2_w_gate, ffn2_w_up,
            ffn2_w_down)
    for layer in range(ffn1_pre_g.shape[0]):
        p = {n: a[layer] for n, a in zip(_NAMES, args)}
        L = x.shape[1]
        x = _layer(x, p, tm_ffn=min(1024, L), tf=1408, tm_out=min(1024, L),
                   TQ=min(256, L), KC=min(512, L))
    return x
```

```python
import functools
import math

import jax
import jax.numpy as jnp
from jax import lax
from jax.experimental import pallas as pl
from jax.experimental.pallas import tpu as pltpu

F32 = jnp.float32
BF16 = jnp.bfloat16
I32 = jnp.int32

LANES = 128
ATT_HEAD_DIM = 64
ATT_HEADS = 8
IDX_HEADS = 4
IDX_DIM = 64
IDX_ROPE_DIM = 32
Q_LORA_RANK = 256
TOPK_MAX = 256
GDN_HEAD_DIM = 128
GDN_HEADS = 4
GDN_CONV = 4
GDN_CHUNK = 64
ROPE_THETA = 10000.0
NORM_EPS = 1e-6
NEG_BIG = -1e30

VMEM_LIMIT = 56 * 1024 * 1024


def _rms(x, g):
    return x * lax.rsqrt(jnp.mean(x * x, axis=-1, keepdims=True) + NORM_EPS) * g


def _dot(a, b):
    return jnp.dot(a, b, preferred_element_type=F32)


def _dot_nt(a, b):
    return lax.dot_general(a, b, (((1,), (1,)), ((), ())), preferred_element_type=F32)


def _ffn_body(x_ref, pre_ref, post_ref, wg_ref, wu_ref, wd_ref, o_ref, xn_ref, acc_ref):
    j = pl.program_id(1)

    @pl.when(j == 0)
    def _():
        xn_ref[...] = _rms(x_ref[...], pre_ref[...]).astype(BF16)
        acc_ref[...] = jnp.zeros_like(acc_ref)

    xn = xn_ref[...]
    g = _dot(xn, wg_ref[...])
    u = _dot(xn, wu_ref[...])
    h = (g * jax.nn.sigmoid(g) * u).astype(BF16)
    acc_ref[...] += _dot(h, wd_ref[...])

    @pl.when(j == pl.num_programs(1) - 1)
    def _():
        o_ref[...] = x_ref[...] + 0.5 * _rms(acc_ref[...], post_ref[...])


def _ffn(x, pre_g, post_g, wg, wu, wd, *, tm, tf):
    T, D = x.shape
    F = wg.shape[1]
    return pl.pallas_call(
        _ffn_body,
        grid=(T // tm, F // tf),
        in_specs=[
            pl.BlockSpec((tm, D), lambda i, j: (i, 0)),
            pl.BlockSpec((1, D), lambda i, j: (0, 0)),
            pl.BlockSpec((1, D), lambda i, j: (0, 0)),
            pl.BlockSpec((D, tf), lambda i, j: (0, j)),
            pl.BlockSpec((D, tf), lambda i, j: (0, j)),
            pl.BlockSpec((tf, D), lambda i, j: (j, 0)),
        ],
        out_specs=pl.BlockSpec((tm, D), lambda i, j: (i, 0)),
        out_shape=jax.ShapeDtypeStruct((T, D), F32),
        scratch_shapes=[pltpu.VMEM((tm, D), BF16), pltpu.VMEM((tm, D), F32)],
        compiler_params=pltpu.CompilerParams(
            dimension_semantics=("parallel", "arbitrary"), vmem_limit_bytes=VMEM_LIMIT),
        name="ffn",
    )(x, pre_g, post_g, wg, wu, wd)


def _rope(x, cos, sin_signed, half):
    n, W = x.shape
    lane = lax.broadcasted_iota(I32, (n, LANES), 1)
    first = (lane & 63) < half
    outs = []
    for s in range(W // LANES):
        xs = x[:, s * LANES:(s + 1) * LANES]
        partner = jnp.where(first, pltpu.roll(xs, LANES - half, 1), pltpu.roll(xs, half, 1))
        outs.append(xs * cos + partner * sin_signed)
    return outs[0] if len(outs) == 1 else jnp.concatenate(outs, axis=1)


def _prep_body(x_ref, pre_ref, watt_ref, wsm_ref, wgdn_ref, wz_ref, qan_ref, wqb_ref, wqib_ref,
               lng_ref, lnb_ref, conv_ref, alog_ref, dtb_ref, cosa_ref, sina_ref, cosi_ref, sini_ref,
               q_o, k_o, vt_o, qi_o, ki_o, w_o, gq_o, gk_o, gv_o, z_o, beta_o, g_o, kn_o,
               pbuf, *, tm):
    tile = pl.program_id(1)
    h = _rms(x_ref[0], pre_ref[...]).astype(BF16)
    cosa, sina = cosa_ref[...], sina_ref[...]
    cosi, sini = cosi_ref[...], sini_ref[...]

    pa = _dot(h, watt_ref[...])
    cq = _rms(pa[:, :Q_LORA_RANK], qan_ref[...]).astype(BF16)
    q = _rope(_dot(cq, wqb_ref[...]), cosa, sina, ATT_HEAD_DIM // 2)
    q_o[0] = (q * (ATT_HEAD_DIM ** -0.5)).astype(BF16)
    qi_o[0] = _rope(_dot(cq, wqib_ref[...]), cosi, sini, IDX_ROPE_DIM // 2).astype(BF16)
    k_bf = _rope(pa[:, 256:768], cosa, sina, ATT_HEAD_DIM // 2).astype(BF16)
    k_o[0] = k_bf
    vt_o[0, 0] = pa[:, 768:1280].T.astype(BF16)

    lane = lax.broadcasted_iota(I32, (tm, LANES), 1)
    real = lane < IDX_DIM
    lower = lane < ATT_HEAD_DIM
    k_sq = jnp.square(k_bf.astype(F32))
    head_row = lax.broadcasted_iota(I32, (ATT_HEADS, LANES), 0)
    kn = jnp.zeros((ATT_HEADS, LANES), F32)
    for hh in range(ATT_HEADS):
        slab = k_sq[:, (hh // 2) * LANES:(hh // 2 + 1) * LANES]
        norm_sq = jnp.sum(jnp.where(lower if hh % 2 == 0 else ~lower, slab, 0.0), axis=1, keepdims=True)
        kn = jnp.where(head_row == hh, jnp.max(norm_sq, axis=0, keepdims=True), kn)
    kn_o[0, 0] = kn

    ki = pa[:, 1280:1408]
    mu = jnp.sum(jnp.where(real, ki, 0.0), axis=-1, keepdims=True) * (1.0 / IDX_DIM)
    xc = jnp.where(real, ki - mu, 0.0)
    var = jnp.sum(xc * xc, axis=-1, keepdims=True) * (1.0 / IDX_DIM)
    kn = xc * lax.rsqrt(var + NORM_EPS) * lng_ref[...] + lnb_ref[...]
    kr = _rope(kn, cosi, sini, IDX_ROPE_DIM // 2)
    ki_o[0] = jnp.where(real, kr, pltpu.roll(kr, IDX_DIM, 1)).astype(BF16)

    ps = _dot(h, wsm_ref[...])
    w_o[0] = ps[:, :LANES]
    beta_o[0] = jax.nn.sigmoid(ps[:, LANES:2 * LANES])
    xa = ps[:, 2 * LANES:] + dtb_ref[...]
    softplus = jnp.maximum(xa, 0.0) + jnp.log(1.0 + jnp.exp(-jnp.abs(xa)))
    g_o[0] = -jnp.exp(alog_ref[...]) * softplus

    pg = _dot(h, wgdn_ref[...])

    @pl.when(tile == 0)
    def _():
        pbuf[0:8, :] = jnp.zeros((8, pbuf.shape[1]), F32)

    @pl.when(tile > 0)
    def _():
        pbuf[0:8, :] = pbuf[tm:tm + 8, :]

    pbuf[8:tm + 8, :] = pg
    conv = pbuf[5:5 + tm, :] * conv_ref[0:1, :]
    conv = conv + pbuf[6:6 + tm, :] * conv_ref[1:2, :]
    conv = conv + pbuf[7:7 + tm, :] * conv_ref[2:3, :]
    conv = conv + pg * conv_ref[3:4, :]
    act = conv * jax.nn.sigmoid(conv)
    W = GDN_HEADS * GDN_HEAD_DIM
    for hd in range(GDN_HEADS):
        sl = slice(hd * GDN_HEAD_DIM, (hd + 1) * GDN_HEAD_DIM)
        xq = act[:, hd * GDN_HEAD_DIM:(hd + 1) * GDN_HEAD_DIM]
        xk = act[:, W + hd * GDN_HEAD_DIM:W + (hd + 1) * GDN_HEAD_DIM]
        qn = xq * lax.rsqrt(jnp.sum(xq * xq, axis=-1, keepdims=True) + NORM_EPS)
        gq_o[0, :, sl] = qn * (GDN_HEAD_DIM ** -0.5)
        gk_o[0, :, sl] = xk * lax.rsqrt(jnp.sum(xk * xk, axis=-1, keepdims=True) + NORM_EPS)
    gv_o[0] = act[:, 2 * W:]
    z_o[0] = _dot(h, wz_ref[...])


def _prep(x, pre_g, watt, wsm, wgdn, wz, qan, wqb, wqib, lng, lnb, convw, alog, dtb,
          cosa, sina, cosi, sini, *, tm):
    B, L, D = x.shape
    n = L // tm
    full = lambda a: pl.BlockSpec(a.shape, lambda b, i: (0, 0))
    tab = pl.BlockSpec((tm, LANES), lambda b, i: (i, 0))
    tok = lambda w: pl.BlockSpec((1, tm, w), lambda b, i: (b, i, 0))
    vt_spec = pl.BlockSpec((1, 1, 512, tm), lambda b, i: (b, i, 0, 0))
    kn_spec = pl.BlockSpec((1, 1, ATT_HEADS, LANES), lambda b, i: (b, i, 0, 0))
    outs = [(512, BF16), (512, BF16), (512, BF16), (256, BF16), (LANES, BF16), (LANES, F32),
            (512, F32), (512, F32), (512, F32), (512, F32), (LANES, F32), (LANES, F32)]
    return pl.pallas_call(
        functools.partial(_prep_body, tm=tm),
        grid=(B, n),
        in_specs=[tok(D), full(pre_g), full(watt), full(wsm), full(wgdn), full(wz), full(qan),
                  full(wqb), full(wqib), full(lng), full(lnb), full(convw), full(alog), full(dtb),
                  tab, tab, tab, tab],
        out_specs=[vt_spec if idx == 2 else tok(w) for idx, (w, _) in enumerate(outs)] + [kn_spec],
        out_shape=[jax.ShapeDtypeStruct((B, n, 512, tm) if idx == 2 else (B, L, w), dt)
                   for idx, (w, dt) in enumerate(outs)]
                  + [jax.ShapeDtypeStruct((B, n, ATT_HEADS, LANES), F32)],
        scratch_shapes=[pltpu.VMEM((tm + 8, wgdn.shape[1]), F32)],
        compiler_params=pltpu.CompilerParams(
            dimension_semantics=("arbitrary", "arbitrary"), vmem_limit_bytes=VMEM_LIMIT),
        name="prep",
    )(x, pre_g, watt, wsm, wgdn, wz, qan, wqb, wqib, lng, lnb, convw, alog, dtb,
      cosa, sina, cosi, sini)


def _fold8(x, op, ways=4):
    parts = [x[r:r + 8] for r in range(0, x.shape[0], 8)]
    accs = parts[:ways]
    for idx, part in enumerate(parts[ways:]):
        accs[idx % ways] = op(accs[idx % ways], part)
    while len(accs) > 1:
        accs = [op(a, b) for a, b in zip(accs[0::2], accs[1::2])] + ([accs[-1]] if len(accs) % 2 else [])
    return accs[0]


def _attn_body(q_ref, qi_ref, w_ref, k_ref, vt_ref, ki_ref, kn_ref, o_ref,
               sc_scr, st_scr, m_scr, acc_scr, *, TQ, KC, L, top_k):
    i = pl.program_id(1)
    t0 = i * TQ
    nch = (t0 + TQ + KC - 1) // KC
    qpos = t0 + lax.broadcasted_iota(I32, (1, TQ), 1)
    lane = lax.broadcasted_iota(I32, (TQ, LANES), 1)
    lower = lane < ATT_HEAD_DIM
    zero_bf = jnp.zeros((TQ, LANES), BF16)
    rmc = lax.broadcasted_iota(I32, (KC, TQ), 0) - lax.broadcasted_iota(I32, (KC, TQ), 1)

    def head_halves(ref, n_heads):
        out = []
        for hh in range(n_heads):
            slab = ref[0, :, (hh // 2) * LANES:(hh // 2 + 1) * LANES]
            out.append(jnp.where(lower if hh % 2 == 0 else ~lower, slab, zero_bf))
        return out

    wt = w_ref[0].T
    wrow = [wt[hh:hh + 1, :] for hh in range(IDX_HEADS)]
    qim = head_halves(qi_ref, IDX_HEADS)

    def p1(j, carry):
        lo, hi = carry
        kk = ki_ref[0, pl.ds(pl.multiple_of(j * KC, KC), KC), :]
        sc = jnp.zeros((KC, TQ), F32)
        for hh in range(IDX_HEADS):
            sc = sc + wrow[hh] * jnp.maximum(_dot_nt(kk, qim[hh]), 0.0)
        masked = jnp.where(rmc <= t0 - j * KC, sc, -jnp.inf)
        sc_scr[j] = masked
        lo = jnp.minimum(lo, jnp.min(_fold8(sc, jnp.minimum), axis=0, keepdims=True))
        hi = jnp.maximum(hi, jnp.max(_fold8(masked, jnp.maximum), axis=0, keepdims=True))
        return lo, hi

    lo, top = lax.fori_loop(0, nch, p1, (jnp.full((1, TQ), jnp.inf, F32), jnp.full((1, TQ), -jnp.inf, F32)))

    krow = jnp.minimum(top_k, qpos + 1).astype(F32)

    def count(pred, ways=4):
        def body(j, accs):
            accs = list(accs)
            for r in range(KC // 8):
                hit = pred(sc_scr[j, r * 8:(r + 1) * 8, :], j * KC + r * 8)
                accs[r % ways] = accs[r % ways] + jnp.where(hit, 1.0, 0.0)
            return tuple(accs)
        accs = lax.fori_loop(0, nch, body, tuple(jnp.zeros((8, TQ), F32) for _ in range(ways)))
        return jnp.sum(functools.reduce(jnp.add, accs), axis=0, keepdims=True)

    ncausal = (qpos + 1).astype(F32)
    hi = top + (jnp.abs(top) * (2.0 ** -20) + 1e-30)
    c_ge0 = count(lambda blk, k0: blk >= 0.0)
    c_gt0 = count(lambda blk, k0: blk > 0.0)
    all_rows = ncausal == krow
    at_zero = (c_gt0 < krow) & (c_ge0 >= krow) & ~all_rows
    up = (c_ge0 >= krow) & (lo < 0.0)
    down = (c_ge0 < krow) & (hi > 0.0)
    LO, HI, CLO, CHI, DONE = range(5)
    st_scr[LO:LO + 1, :] = jnp.where(up, 0.0, lo)
    st_scr[CLO:CLO + 1, :] = jnp.where(up, c_ge0, ncausal)
    st_scr[HI:HI + 1, :] = jnp.where(down, 0.0, hi)
    st_scr[CHI:CHI + 1, :] = jnp.where(at_zero, c_gt0, jnp.where(down, c_ge0, 0.0))
    done0 = jnp.where(at_zero | all_rows, 1.0, 0.0)
    st_scr[DONE:DONE + 1, :] = done0

    def bisect(active):
        lo, hi, done = st_scr[LO:LO + 1, :], st_scr[HI:HI + 1, :], st_scr[DONE:DONE + 1, :]
        mid = 0.5 * lo + 0.5 * hi
        stuck = (mid <= lo) | (mid >= hi)
        c = count(lambda blk, k0: blk >= mid)
        ge = c >= krow
        upd = (done == 0.0) & ~stuck
        st_scr[LO:LO + 1, :] = jnp.where(upd & ge, mid, lo)
        st_scr[CLO:CLO + 1, :] = jnp.where(upd & ge, c, st_scr[CLO:CLO + 1, :])
        st_scr[HI:HI + 1, :] = jnp.where(upd & ~ge, mid, hi)
        st_scr[CHI:CHI + 1, :] = jnp.where(upd & ~ge, c, st_scr[CHI:CHI + 1, :])
        new_done = jnp.where(stuck | (upd & (c == krow)), 1.0, done)
        st_scr[DONE:DONE + 1, :] = new_done
        return jnp.max(jnp.where(new_done == 0.0, 1.0, 0.0)).astype(I32)

    lax.while_loop(lambda a: a > 0, bisect, jnp.max(jnp.where(done0 == 0.0, 1.0, 0.0)).astype(I32))

    thr = st_scr[LO:LO + 1, :]
    tie = st_scr[CLO:CLO + 1, :] > krow
    need = krow - st_scr[CHI:CHI + 1, :]

    @pl.when(jnp.max(jnp.where(tie, 1.0, 0.0)) > 0.0)
    def _():
        below = jnp.where(lax.broadcasted_iota(I32, (KC, KC), 1) < lax.broadcasted_iota(I32, (KC, KC), 0),
                          1.0, 0.0).astype(BF16)

        def drop(j, seen):
            blk = sc_scr[j]
            tied = (blk == thr) & tie
            ind = jnp.where(tied, 1.0, 0.0)
            rank = _dot(below, ind.astype(BF16)) + seen
            sc_scr[j] = jnp.where(tied & (rank >= need), -jnp.inf, blk)
            return rank[KC - 1:KC, :] + ind[KC - 1:KC, :]

        lax.fori_loop(0, nch, drop, jnp.zeros((1, TQ), F32))

    qm = head_halves(q_ref, ATT_HEADS)
    HD = ATT_HEAD_DIM
    heads = range(ATT_HEADS)
    ones_rows = jnp.ones((16, KC), BF16)

    def chunk_operands(j):
        start = pl.multiple_of(j * KC, KC)
        bias = jnp.where(sc_scr[j] >= thr, 0.0, NEG_BIG)
        kp = [k_ref[0, pl.ds(start, KC), pr * LANES:(pr + 1) * LANES] for pr in range(ATT_HEADS // 2)]
        vt = [jnp.concatenate([vt_ref[0, j, hh * HD:(hh + 1) * HD, :], ones_rows], axis=0) for hh in heads]
        return bias, kp, vt

    qt = q_ref[0].astype(F32).T
    in_reach = lax.broadcasted_iota(I32, kn_ref.shape[1:], 0) < nch
    kmax = jnp.max(jnp.where(in_reach, kn_ref[0], 0.0), axis=0)
    shift = []
    for hh in heads:
        qh = qt[hh * HD:(hh + 1) * HD, :]
        shift.append(jnp.sqrt(jnp.sum(qh * qh, axis=0, keepdims=True) * kmax[hh:hh + 1, 0:1]))
    acc_scr[...] = jnp.zeros(acc_scr.shape, F32)

    def p3_fast(j, carry):
        bias, kp, vt = chunk_operands(j)
        acc_old = [acc_scr[hh] for hh in heads]
        p = [jnp.exp((_dot_nt(kp[hh // 2], qm[hh]) + bias - shift[hh]).astype(BF16)) for hh in heads]
        pv = [_dot(vt[hh], p[hh]) for hh in heads]
        for hh in heads:
            acc_scr[hh] = acc_old[hh] + pv[hh]
        return carry

    lax.fori_loop(0, nch, p3_fast, 0)

    lsum = [acc_scr[hh, HD:HD + 1, :] for hh in heads]
    weak = functools.reduce(jnp.logical_or, [~(x >= 1e-25) for x in lsum])

    @pl.when(jnp.max(jnp.where(weak, 1.0, 0.0)) > 0.0)
    def _():
        m_scr[...] = jnp.full(m_scr.shape, NEG_BIG, F32)
        acc_scr[...] = jnp.zeros(acc_scr.shape, F32)

        def p3_safe(j, carry):
            bias, kp, vt = chunk_operands(j)
            m_old = [m_scr[hh:hh + 1, :] for hh in heads]
            acc_old = [acc_scr[hh] for hh in heads]
            s = [_dot_nt(kp[hh // 2], qm[hh]) + bias for hh in heads]
            m_new = [jnp.maximum(m_old[hh], jnp.max(_fold8(s[hh], jnp.maximum), axis=0, keepdims=True))
                     for hh in heads]
            p = [jnp.exp((s[hh] - m_new[hh]).astype(BF16)) for hh in heads]
            pv = [_dot(vt[hh], p[hh]) for hh in heads]
            for hh in heads:
                m_scr[hh:hh + 1, :] = m_new[hh]
                acc_scr[hh] = jnp.exp(m_old[hh] - m_new[hh]) * acc_old[hh] + pv[hh]
            return carry

        lax.fori_loop(0, nch, p3_safe, 0)

    for pr in range(ATT_HEADS // 2):
        halves = [acc_scr[hh, 0:HD, :] / acc_scr[hh, HD:HD + 1, :] for hh in (2 * pr, 2 * pr + 1)]
        o_ref[0, :, pr * LANES:(pr + 1) * LANES] = jnp.concatenate(halves, axis=0).T.astype(o_ref.dtype)


def _attention(q, qi, w, k, vt, ki, kn, *, TQ, KC):
    B, L, _ = q.shape
    top_k = min(TOPK_MAX, L // 4)
    nq = L // TQ
    blk = lambda width: pl.BlockSpec((1, TQ, width), lambda b, i: (b, i, 0))
    whole = lambda width: pl.BlockSpec((1, L, width), lambda b, i: (b, 0, 0),
                                       pipeline_mode=pl.Buffered(1))
    vt_spec = pl.BlockSpec((1, L // KC, 512, KC), lambda b, i: (b, 0, 0, 0), pipeline_mode=pl.Buffered(1))
    kn_spec = pl.BlockSpec((1, L // KC, ATT_HEADS, LANES), lambda b, i: (b, 0, 0, 0))
    return pl.pallas_call(
        functools.partial(_attn_body, TQ=TQ, KC=KC, L=L, top_k=top_k),
        grid=(B, nq),
        in_specs=[blk(512), blk(256), blk(LANES), whole(512), vt_spec, whole(LANES), kn_spec],
        out_specs=blk(512),
        out_shape=jax.ShapeDtypeStruct((B, L, 512), BF16),
        scratch_shapes=[
            pltpu.VMEM((L // KC, KC, TQ), F32),
            pltpu.VMEM((8, TQ), F32),
            pltpu.VMEM((ATT_HEADS, TQ), F32),
            pltpu.VMEM((ATT_HEADS, ATT_HEAD_DIM + 16, TQ), F32),
        ],
        compiler_params=pltpu.CompilerParams(
            dimension_semantics=("arbitrary", "arbitrary"), vmem_limit_bytes=VMEM_LIMIT),
        name="dsa_attention",
    )(q, qi, w, k, vt, ki, kn)


def _mm(a, b):
    return jnp.dot(a.astype(BF16), b.astype(BF16), preferred_element_type=F32)


def _gdn_body(q_ref, k_ref, v_ref, z_ref, g_ref, b_ref, gn_ref, o_ref, s_scr, *, nb, G):
    C = GDN_CHUNK

    @pl.when(pl.program_id(0) == 0)
    def _():
        s_scr[...] = jnp.zeros_like(s_scr)

    ri = lax.broadcasted_iota(I32, (C, C), 0)
    ci = lax.broadcasted_iota(I32, (C, C), 1)
    incl = ci <= ri
    strict = ci < ri
    eye = ci == ri
    lane = lax.broadcasted_iota(I32, (C, LANES), 1)
    chains = [(b, hd) for b in range(nb) for hd in range(GDN_HEADS)]
    units = [(gi, b, hd) for gi in range(G) for b, hd in chains]
    n = range(len(units))
    rows = [slice(gi * C, (gi + 1) * C) for gi, _, _ in units]
    cols = [slice(hd * GDN_HEAD_DIM, (hd + 1) * GDN_HEAD_DIM) for _, _, hd in units]
    q = [q_ref[b, rows[i], cols[i]] for i, (_, b, _) in enumerate(units)]
    k = [k_ref[b, rows[i], cols[i]] for i, (_, b, _) in enumerate(units)]
    v = [v_ref[b, rows[i], cols[i]] for i, (_, b, _) in enumerate(units)]
    z = [z_ref[b, rows[i], cols[i]] for i, (_, b, _) in enumerate(units)]
    g_col = [jnp.sum(jnp.where(lane == hd, g_ref[b, rows[i], :], 0.0), axis=1, keepdims=True)
             for i, (_, b, hd) in enumerate(units)]
    beta = [jnp.sum(jnp.where(lane == hd, b_ref[b, rows[i], :], 0.0), axis=1, keepdims=True)
            for i, (_, b, hd) in enumerate(units)]
    s_in = [s_scr[c] for c in range(len(chains))]
    g_row = [jnp.sum(jnp.where(eye, jnp.broadcast_to(x, (C, C)), 0.0), axis=0, keepdims=True) for x in g_col]
    gc_col = [jnp.sum(jnp.where(incl, jnp.broadcast_to(x, (C, C)), 0.0), axis=1, keepdims=True) for x in g_row]
    gc_row = [jnp.sum(jnp.where(ri <= ci, jnp.broadcast_to(x, (C, C)), 0.0), axis=0, keepdims=True) for x in g_col]
    decay = [jnp.where(incl, jnp.exp(jnp.where(incl, gc_col[i] - gc_row[i], 0.0)), 0.0) for i in n]
    eg = [jnp.exp(x) for x in gc_col]
    g_last = [x[C - 1:C, :] for x in gc_col]
    kb = [k[i] * beta[i] for i in n]
    k_bf = [x.astype(BF16) for x in k]
    a_mat = [jnp.where(strict, _dot_nt(kb[i].astype(BF16), k_bf[i]) * decay[i], 0.0) for i in n]
    qk = [jnp.where(incl, _dot_nt(q[i].astype(BF16), k_bf[i]) * decay[i], 0.0) for i in n]
    t_inv = [jnp.where(eye, 1.0, 0.0) - a for a in a_mat]
    pw = a_mat
    for _ in range(5):
        pw = [_mm(x, x) for x in pw]
        t_inv = [t_inv[i] + _mm(t_inv[i], pw[i]) for i in n]
    u = [_mm(t_inv[i], v[i] * beta[i]) for i in n]
    w = [_mm(t_inv[i], kb[i] * eg[i]) for i in n]
    q_dec = [q[i] * eg[i] for i in n]
    k_tail_t = [(k[i] * jnp.exp(g_last[i] - gc_col[i])).T for i in n]
    a_last = [jnp.exp(x) for x in g_last]
    gate = [x * jax.nn.sigmoid(x) for x in z]
    state = s_in
    outs = []
    for gi in range(G):
        ids = [gi * len(chains) + c for c in range(len(chains))]
        v_new = [u[i] - _mm(w[i], state[c]) for c, i in enumerate(ids)]
        qs = [_mm(q_dec[i], state[c]) for c, i in enumerate(ids)]
        state = [state[c] * a_last[i] + _mm(k_tail_t[i], v_new[c]) for c, i in enumerate(ids)]
        outs += [qs[c] + _mm(qk[i], v_new[c]) for c, i in enumerate(ids)]
    for c in range(len(chains)):
        s_scr[c] = state[c]
    for i, (_, b, _) in enumerate(units):
        o_ref[b, rows[i], cols[i]] = (_rms(outs[i], gn_ref[...]) * gate[i]).astype(o_ref.dtype)


def _gdn(gq, gk, gv, z, g, beta, gn, *, G):
    B, L, W = gq.shape
    R = G * GDN_CHUNK
    tok = lambda width: pl.BlockSpec((B, R, width), lambda n: (0, n, 0))
    return pl.pallas_call(
        functools.partial(_gdn_body, nb=B, G=G),
        grid=(L // R,),
        in_specs=[tok(W), tok(W), tok(W), tok(W), tok(LANES), tok(LANES),
                  pl.BlockSpec(gn.shape, lambda n: (0, 0))],
        out_specs=tok(W),
        out_shape=jax.ShapeDtypeStruct((B, L, W), BF16),
        scratch_shapes=[pltpu.VMEM((B * GDN_HEADS, GDN_HEAD_DIM, GDN_HEAD_DIM), F32)],
        compiler_params=pltpu.CompilerParams(
            dimension_semantics=("arbitrary",), vmem_limit_bytes=VMEM_LIMIT),
        name="gdn",
    )(gq, gk, gv, z, g, beta, gn)


def _out_body(x_ref, a_ref, d_ref, wa_ref, wd_ref, g_ref, o_ref):
    m = _dot(a_ref[...], wa_ref[...]) + _dot(d_ref[...], wd_ref[...])
    o_ref[...] = x_ref[...] + _rms(m, g_ref[...])


def _out_proj(x, att, gdn, wa, wd, post_g, *, tm):
    T, D = x.shape
    W = att.shape[1]
    row = lambda width: pl.BlockSpec((tm, width), lambda i: (i, 0))
    full = lambda a: pl.BlockSpec(a.shape, lambda i: (0, 0))
    return pl.pallas_call(
        _out_body,
        grid=(T // tm,),
        in_specs=[row(D), row(W), row(W), full(wa), full(wd), full(post_g)],
        out_specs=row(D),
        out_shape=jax.ShapeDtypeStruct((T, D), F32),
        compiler_params=pltpu.CompilerParams(
            dimension_semantics=("parallel",), vmem_limit_bytes=VMEM_LIMIT),
        name="out_proj",
    )(x, att, gdn, wa, wd, post_g)


def _rope_tables(L, rot_dim):
    half = rot_dim // 2
    inv_freq = 1.0 / (ROPE_THETA ** (jnp.arange(half, dtype=F32) / half))
    ang = jnp.arange(L, dtype=jnp.int32).astype(F32)[:, None] * inv_freq[None, :]
    cos, sin = jnp.cos(ang), jnp.sin(ang)
    pad = 64 - rot_dim
    cos64 = jnp.concatenate([cos, cos, jnp.ones((L, pad), F32)], axis=1)
    sin64 = jnp.concatenate([-sin, sin, jnp.zeros((L, pad), F32)], axis=1)
    return jnp.tile(cos64, (1, 2)), jnp.tile(sin64, (1, 2))


def _pad_cols(a, width):
    return jnp.pad(a, ((0, 0), (0, width - a.shape[1])))


def _layer(x, p, *, tm_ffn, tf, tm_out, TQ, KC):
    B, L, D = x.shape
    T = B * L
    row = lambda a: a.reshape(1, -1)
    x2 = _ffn(x.reshape(T, D), row(p['ffn1_pre_g']), row(p['ffn1_post_g']),
              p['ffn1_w_gate'].astype(BF16), p['ffn1_w_up'].astype(BF16), p['ffn1_w_down'].astype(BF16),
              tm=tm_ffn, tf=tf)

    w_in = p['w_in']
    sizes = (Q_LORA_RANK, 512, 512, IDX_DIM, IDX_HEADS, 512, 512, 512, 512, GDN_HEADS, GDN_HEADS)
    offs = [0]
    for s in sizes:
        offs.append(offs[-1] + s)
    col = lambda a, b: w_in[:, offs[a]:offs[b]]
    watt = jnp.concatenate([col(0, 3), _pad_cols(col(3, 4), LANES)], axis=1).astype(BF16)
    wsm = jnp.concatenate([_pad_cols(col(4, 5), LANES), _pad_cols(col(9, 10), LANES),
                           _pad_cols(col(10, 11), LANES)], axis=1).astype(BF16)
    wgdn = col(5, 8).astype(BF16)
    wz = col(8, 9).astype(BF16)
    cosa, sina = _rope_tables(L, ATT_HEAD_DIM)
    cosi, sini = _rope_tables(L, IDX_ROPE_DIM)
    q, k, vt, qi, ki, w, gq, gk, gv, z, beta, g, kn = _prep(
        x2.reshape(B, L, D), row(p['mix_pre_g']), watt, wsm, wgdn, wz, row(p['q_a_norm_g']),
        p['w_q_b'].astype(BF16), p['w_qidx_b'].astype(BF16),
        _pad_cols(row(p['kidx_ln_g']), LANES), _pad_cols(row(p['kidx_ln_b']), LANES),
        p['gdn_conv_w'], _pad_cols(row(p['gdn_a_log']), LANES), _pad_cols(row(p['gdn_dt_bias']), LANES),
        cosa, sina, cosi, sini, tm=KC)

    att = _attention(q, qi, w, k, vt, ki, kn, TQ=TQ, KC=KC)
    gdn = _gdn(gq, gk, gv, z, g, beta, row(p['gdn_out_norm_g']), G=4)
    w_out = p['w_out'].astype(BF16)
    x3 = _out_proj(x2, att.reshape(T, -1), gdn.reshape(T, -1), w_out[:512], w_out[512:],
                   row(p['mix_post_g']), tm=tm_out)
    x4 = _ffn(x3, row(p['ffn2_pre_g']), row(p['ffn2_post_g']),
              p['ffn2_w_gate'].astype(BF16), p['ffn2_w_up'].astype(BF16), p['ffn2_w_down'].astype(BF16),
              tm=tm_ffn, tf=tf)
    return x4.reshape(B, L, D)


_NAMES = ('ffn1_pre_g', 'ffn1_post_g', 'ffn1_w_gate', 'ffn1_w_up', 'ffn1_w_down', 'mix_pre_g',
          'mix_post_g', 'w_in', 'q_a_norm_g', 'w_q_b', 'w_qidx_b', 'kidx_ln_g', 'kidx_ln_b',
          'gdn_conv_w', 'gdn_a_log', 'gdn_dt_bias', 'gdn_out_norm_g', 'w_out', 'ffn2_pre_g',
          'ffn2_post_g', 'ffn2_w_gate', 'ffn2_w_up', 'ffn2_w_down')


def kernel(x, ffn1_pre_g, ffn1_post_g, ffn1_w_gate, ffn1_w_up, ffn1_w_down, mix_pre_g, mix_post_g, w_in, q_a_norm_g, w_q_b, w_qidx_b, kidx_ln_g, kidx_ln_b, gdn_conv_w, gdn_a_log, gdn_dt_bias, gdn_out_norm_g, w_out, ffn2_pre_g, ffn2_post_g, ffn2_w_gate, ffn2_w_up, ffn2_w_down):
    args = (ffn1_pre_g, ffn1_post_g, ffn1_w_gate, ffn1_w_up, ffn1_w_down, mix_pre_g, mix_post_g,
            w_in, q_a_norm_g, w_q_b, w_qidx_b, kidx_ln_g, kidx_ln_b, gdn_conv_w, gdn_a_log,
            gdn_dt_bias, gdn_out_norm_g, w_out, ffn2_pre_g, ffn2_post_g, ffn2_w_gate, ffn2_w_up,
            ffn2_w_down)
    for layer in range(ffn1_pre_g.shape[0]):
        p = {n: a[layer] for n, a in zip(_NAMES, args)}
        L = x.shape[1]
        x = _layer(x, p, tm_ffn=min(1024, L), tf=1408, tm_out=min(1024, L),
                   TQ=min(256, L), KC=min(512, L))
    return x
```

```python
import functools
import math

import jax
import jax.numpy as jnp
from jax import lax
from jax.experimental import pallas as pl
from jax.experimental.pallas import tpu as pltpu

F32 = jnp.float32
BF16 = jnp.bfloat16
I32 = jnp.int32

LANES = 128
ATT_HEAD_DIM = 64
ATT_HEADS = 8
IDX_HEADS = 4
IDX_DIM = 64
IDX_ROPE_DIM = 32
Q_LORA_RANK = 256
TOPK_MAX = 256
GDN_HEAD_DIM = 128
GDN_HEADS = 4
GDN_CONV = 4
GDN_CHUNK = 64
ROPE_THETA = 10000.0
NORM_EPS = 1e-6
NEG_BIG = -1e30

VMEM_LIMIT = 56 * 1024 * 1024


def _rms(x, g):
    return x * lax.rsqrt(jnp.mean(x * x, axis=-1, keepdims=True) + NORM_EPS) * g


def _dot(a, b):
    return jnp.dot(a, b, preferred_element_type=F32)


def _dot_nt(a, b):
    return lax.dot_general(a, b, (((1,), (1,)), ((), ())), preferred_element_type=F32)


def _ffn_body(x_ref, pre_ref, post_ref, wg_ref, wu_ref, wd_ref, o_ref, xn_ref, acc_ref):
    j = pl.program_id(1)

    @pl.when(j == 0)
    def _():
        xn_ref[...] = _rms(x_ref[...], pre_ref[...]).astype(BF16)
        acc_ref[...] = jnp.zeros_like(acc_ref)

    xn = xn_ref[...]
    g = _dot(xn, wg_ref[...])
    u = _dot(xn, wu_ref[...])
    h = (g * jax.nn.sigmoid(g) * u).astype(BF16)
    acc_ref[...] += _dot(h, wd_ref[...])

    @pl.when(j == pl.num_programs(1) - 1)
    def _():
        o_ref[...] = x_ref[...] + 0.5 * _rms(acc_ref[...], post_ref[...])


def _ffn(x, pre_g, post_g, wg, wu, wd, *, tm, tf):
    T, D = x.shape
    F = wg.shape[1]
    mode = dict(pipeline_mode=pl.Buffered(1)) if tf == F else {}
    return pl.pallas_call(
        _ffn_body,
        grid=(T // tm, F // tf),
        in_specs=[
            pl.BlockSpec((tm, D), lambda i, j: (i, 0)),
            pl.BlockSpec((1, D), lambda i, j: (0, 0)),
            pl.BlockSpec((1, D), lambda i, j: (0, 0)),
            pl.BlockSpec((D, tf), lambda i, j: (0, j), **mode),
            pl.BlockSpec((D, tf), lambda i, j: (0, j), **mode),
            pl.BlockSpec((tf, D), lambda i, j: (j, 0), **mode),
        ],
        out_specs=pl.BlockSpec((tm, D), lambda i, j: (i, 0)),
        out_shape=jax.ShapeDtypeStruct((T, D), F32),
        scratch_shapes=[pltpu.VMEM((tm, D), BF16), pltpu.VMEM((tm, D), F32)],
        compiler_params=pltpu.CompilerParams(
            dimension_semantics=("parallel", "arbitrary"), vmem_limit_bytes=VMEM_LIMIT),
        name="ffn",
    )(x, pre_g, post_g, wg, wu, wd)


def _rope(x, cos, sin_signed, half):
    n, W = x.shape
    lane = lax.broadcasted_iota(I32, (n, LANES), 1)
    first = (lane & 63) < half
    outs = []
    for s in range(W // LANES):
        xs = x[:, s * LANES:(s + 1) * LANES]
        partner = jnp.where(first, pltpu.roll(xs, LANES - half, 1), pltpu.roll(xs, half, 1))
        outs.append(xs * cos + partner * sin_signed)
    return outs[0] if len(outs) == 1 else jnp.concatenate(outs, axis=1)


def _prep_body(x_ref, pre_ref, watt_ref, wsm_ref, wgdn_ref, wz_ref, qan_ref, wqb_ref, wqib_ref,
               lng_ref, lnb_ref, conv_ref, alog_ref, dtb_ref, cosa_ref, sina_ref, cosi_ref, sini_ref,
               q_o, k_o, vt_o, qi_o, ki_o, w_o, gq_o, gk_o, gv_o, z_o, beta_o, g_o, kn_o,
               pbuf, *, tm):
    tile = pl.program_id(1)
    h = _rms(x_ref[0], pre_ref[...]).astype(BF16)
    cosa, sina = cosa_ref[...], sina_ref[...]
    cosi, sini = cosi_ref[...], sini_ref[...]

    pa = _dot(h, watt_ref[...])
    cq = _rms(pa[:, :Q_LORA_RANK], qan_ref[...]).astype(BF16)
    q = _rope(_dot(cq, wqb_ref[...]), cosa, sina, ATT_HEAD_DIM // 2)
    q_o[0] = (q * (ATT_HEAD_DIM ** -0.5)).astype(BF16)
    qi_o[0] = _rope(_dot(cq, wqib_ref[...]), cosi, sini, IDX_ROPE_DIM // 2).astype(BF16)
    k_bf = _rope(pa[:, 256:768], cosa, sina, ATT_HEAD_DIM // 2).astype(BF16)
    k_o[0] = k_bf
    vt_o[0, 0] = pa[:, 768:1280].T.astype(BF16)

    lane = lax.broadcasted_iota(I32, (tm, LANES), 1)
    real = lane < IDX_DIM
    lower = lane < ATT_HEAD_DIM
    k_sq = jnp.square(k_bf.astype(F32))
    head_row = lax.broadcasted_iota(I32, (ATT_HEADS, LANES), 0)
    kn = jnp.zeros((ATT_HEADS, LANES), F32)
    for hh in range(ATT_HEADS):
        slab = k_sq[:, (hh // 2) * LANES:(hh // 2 + 1) * LANES]
        norm_sq = jnp.sum(jnp.where(lower if hh % 2 == 0 else ~lower, slab, 0.0), axis=1, keepdims=True)
        kn = jnp.where(head_row == hh, jnp.max(norm_sq, axis=0, keepdims=True), kn)
    kn_o[0, 0] = kn

    ki = pa[:, 1280:1408]
    mu = jnp.sum(jnp.where(real, ki, 0.0), axis=-1, keepdims=True) * (1.0 / IDX_DIM)
    xc = jnp.where(real, ki - mu, 0.0)
    var = jnp.sum(xc * xc, axis=-1, keepdims=True) * (1.0 / IDX_DIM)
    kn = xc * lax.rsqrt(var + NORM_EPS) * lng_ref[...] + lnb_ref[...]
    kr = _rope(kn, cosi, sini, IDX_ROPE_DIM // 2)
    ki_o[0] = jnp.where(real, kr, pltpu.roll(kr, IDX_DIM, 1)).astype(BF16)

    ps = _dot(h, wsm_ref[...])
    w_o[0] = ps[:, :LANES]
    beta_o[0] = jax.nn.sigmoid(ps[:, LANES:2 * LANES])
    xa = ps[:, 2 * LANES:] + dtb_ref[...]
    softplus = jnp.maximum(xa, 0.0) + jnp.log(1.0 + jnp.exp(-jnp.abs(xa)))
    g_o[0] = -jnp.exp(alog_ref[...]) * softplus

    pg = _dot(h, wgdn_ref[...])

    @pl.when(tile == 0)
    def _():
        pbuf[0:8, :] = jnp.zeros((8, pbuf.shape[1]), F32)

    @pl.when(tile > 0)
    def _():
        pbuf[0:8, :] = pbuf[tm:tm + 8, :]

    pbuf[8:tm + 8, :] = pg
    conv = pbuf[5:5 + tm, :] * conv_ref[0:1, :]
    conv = conv + pbuf[6:6 + tm, :] * conv_ref[1:2, :]
    conv = conv + pbuf[7:7 + tm, :] * conv_ref[2:3, :]
    conv = conv + pg * conv_ref[3:4, :]
    act = conv * jax.nn.sigmoid(conv)
    W = GDN_HEADS * GDN_HEAD_DIM
    for hd in range(GDN_HEADS):
        sl = slice(hd * GDN_HEAD_DIM, (hd + 1) * GDN_HEAD_DIM)
        xq = act[:, hd * GDN_HEAD_DIM:(hd + 1) * GDN_HEAD_DIM]
        xk = act[:, W + hd * GDN_HEAD_DIM:W + (hd + 1) * GDN_HEAD_DIM]
        qn = xq * lax.rsqrt(jnp.sum(xq * xq, axis=-1, keepdims=True) + NORM_EPS)
        gq_o[0, :, sl] = qn * (GDN_HEAD_DIM ** -0.5)
        gk_o[0, :, sl] = xk * lax.rsqrt(jnp.sum(xk * xk, axis=-1, keepdims=True) + NORM_EPS)
    gv_o[0] = act[:, 2 * W:]
    z_o[0] = _dot(h, wz_ref[...])


def _prep(x, pre_g, watt, wsm, wgdn, wz, qan, wqb, wqib, lng, lnb, convw, alog, dtb,
          cosa, sina, cosi, sini, *, tm):
    B, L, D = x.shape
    n = L // tm
    full = lambda a: pl.BlockSpec(a.shape, lambda b, i: (0, 0))
    tab = pl.BlockSpec((tm, LANES), lambda b, i: (i, 0))
    tok = lambda w: pl.BlockSpec((1, tm, w), lambda b, i: (b, i, 0))
    vt_spec = pl.BlockSpec((1, 1, 512, tm), lambda b, i: (b, i, 0, 0))
    kn_spec = pl.BlockSpec((1, 1, ATT_HEADS, LANES), lambda b, i: (b, i, 0, 0))
    outs = [(512, BF16), (512, BF16), (512, BF16), (256, BF16), (LANES, BF16), (LANES, F32),
            (512, F32), (512, F32), (512, F32), (512, F32), (LANES, F32), (LANES, F32)]
    return pl.pallas_call(
        functools.partial(_prep_body, tm=tm),
        grid=(B, n),
        in_specs=[tok(D), full(pre_g), full(watt), full(wsm), full(wgdn), full(wz), full(qan),
                  full(wqb), full(wqib), full(lng), full(lnb), full(convw), full(alog), full(dtb),
                  tab, tab, tab, tab],
        out_specs=[vt_spec if idx == 2 else tok(w) for idx, (w, _) in enumerate(outs)] + [kn_spec],
        out_shape=[jax.ShapeDtypeStruct((B, n, 512, tm) if idx == 2 else (B, L, w), dt)
                   for idx, (w, dt) in enumerate(outs)]
                  + [jax.ShapeDtypeStruct((B, n, ATT_HEADS, LANES), F32)],
        scratch_shapes=[pltpu.VMEM((tm + 8, wgdn.shape[1]), F32)],
        compiler_params=pltpu.CompilerParams(
            dimension_semantics=("arbitrary", "arbitrary"), vmem_limit_bytes=VMEM_LIMIT),
        name="prep",
    )(x, pre_g, watt, wsm, wgdn, wz, qan, wqb, wqib, lng, lnb, convw, alog, dtb,
      cosa, sina, cosi, sini)


def _fold8(x, op, ways=4):
    parts = [x[r:r + 8] for r in range(0, x.shape[0], 8)]
    accs = parts[:ways]
    for idx, part in enumerate(parts[ways:]):
        accs[idx % ways] = op(accs[idx % ways], part)
    while len(accs) > 1:
        accs = [op(a, b) for a, b in zip(accs[0::2], accs[1::2])] + ([accs[-1]] if len(accs) % 2 else [])
    return accs[0]


def _attn_body(q_ref, qi_ref, w_ref, k_ref, vt_ref, ki_ref, kn_ref, o_ref,
               sc_scr, st_scr, m_scr, acc_scr, *, TQ, KC, L, top_k):
    i = pl.program_id(1)
    t0 = i * TQ
    nch = (t0 + TQ + KC - 1) // KC
    qpos = t0 + lax.broadcasted_iota(I32, (1, TQ), 1)
    lane = lax.broadcasted_iota(I32, (TQ, LANES), 1)
    lower = lane < ATT_HEAD_DIM
    zero_bf = jnp.zeros((TQ, LANES), BF16)
    rmc = lax.broadcasted_iota(I32, (KC, TQ), 0) - lax.broadcasted_iota(I32, (KC, TQ), 1)

    def head_halves(ref, n_heads):
        out = []
        for hh in range(n_heads):
            slab = ref[0, :, (hh // 2) * LANES:(hh // 2 + 1) * LANES]
            out.append(jnp.where(lower if hh % 2 == 0 else ~lower, slab, zero_bf))
        return out

    wt = w_ref[0].T
    wrow = [wt[hh:hh + 1, :] for hh in range(IDX_HEADS)]
    qim = head_halves(qi_ref, IDX_HEADS)

    def p1(j, carry):
        lo, hi = carry
        kk = ki_ref[0, pl.ds(pl.multiple_of(j * KC, KC), KC), :]
        sc = jnp.zeros((KC, TQ), F32)
        for hh in range(IDX_HEADS):
            sc = sc + wrow[hh] * jnp.maximum(_dot_nt(kk, qim[hh]), 0.0)
        masked = jnp.where(rmc <= t0 - j * KC, sc, -jnp.inf)
        sc_scr[j] = masked
        lo = jnp.minimum(lo, jnp.min(_fold8(sc, jnp.minimum), axis=0, keepdims=True))
        hi = jnp.maximum(hi, jnp.max(_fold8(masked, jnp.maximum), axis=0, keepdims=True))
        return lo, hi

    lo, top = lax.fori_loop(0, nch, p1, (jnp.full((1, TQ), jnp.inf, F32), jnp.full((1, TQ), -jnp.inf, F32)))

    krow = jnp.minimum(top_k, qpos + 1).astype(F32)

    def count(pred, ways=4):
        def body(j, accs):
            accs = list(accs)
            for r in range(KC // 8):
                hit = pred(sc_scr[j, r * 8:(r + 1) * 8, :], j * KC + r * 8)
                accs[r % ways] = accs[r % ways] + jnp.where(hit, 1.0, 0.0)
            return tuple(accs)
        accs = lax.fori_loop(0, nch, body, tuple(jnp.zeros((8, TQ), F32) for _ in range(ways)))
        return jnp.sum(functools.reduce(jnp.add, accs), axis=0, keepdims=True)

    ncausal = (qpos + 1).astype(F32)
    hi = top + (jnp.abs(top) * (2.0 ** -20) + 1e-30)
    c_ge0 = count(lambda blk, k0: blk >= 0.0)
    c_gt0 = count(lambda blk, k0: blk > 0.0)
    all_rows = ncausal == krow
    at_zero = (c_gt0 < krow) & (c_ge0 >= krow) & ~all_rows
    up = (c_ge0 >= krow) & (lo < 0.0)
    down = (c_ge0 < krow) & (hi > 0.0)
    LO, HI, CLO, CHI, DONE = range(5)
    st_scr[LO:LO + 1, :] = jnp.where(up, 0.0, lo)
    st_scr[CLO:CLO + 1, :] = jnp.where(up, c_ge0, ncausal)
    st_scr[HI:HI + 1, :] = jnp.where(down, 0.0, hi)
    st_scr[CHI:CHI + 1, :] = jnp.where(at_zero, c_gt0, jnp.where(down, c_ge0, 0.0))
    done0 = jnp.where(at_zero | all_rows, 1.0, 0.0)
    st_scr[DONE:DONE + 1, :] = done0

    def halve():
        lo, hi, done = st_scr[LO:LO + 1, :], st_scr[HI:HI + 1, :], st_scr[DONE:DONE + 1, :]
        mid = 0.5 * lo + 0.5 * hi
        stuck = (mid <= lo) | (mid >= hi)
        c = count(lambda blk, k0: blk >= mid)
        ge = c >= krow
        upd = (done == 0.0) & ~stuck
        st_scr[LO:LO + 1, :] = jnp.where(upd & ge, mid, lo)
        st_scr[CLO:CLO + 1, :] = jnp.where(upd & ge, c, st_scr[CLO:CLO + 1, :])
        st_scr[HI:HI + 1, :] = jnp.where(upd & ~ge, mid, hi)
        st_scr[CHI:CHI + 1, :] = jnp.where(upd & ~ge, c, st_scr[CHI:CHI + 1, :])
        new_done = jnp.where(stuck | (upd & (c == krow)), 1.0, done)
        st_scr[DONE:DONE + 1, :] = new_done
        return new_done

    def bisect(active):
        halve()
        return jnp.max(jnp.where(halve() == 0.0, 1.0, 0.0)).astype(I32)

    lax.while_loop(lambda a: a > 0, bisect, jnp.max(jnp.where(done0 == 0.0, 1.0, 0.0)).astype(I32))

    thr = st_scr[LO:LO + 1, :]
    tie = st_scr[CLO:CLO + 1, :] > krow
    need = krow - st_scr[CHI:CHI + 1, :]

    @pl.when(jnp.max(jnp.where(tie, 1.0, 0.0)) > 0.0)
    def _():
        SB = min(128, KC)
        below = jnp.where(lax.broadcasted_iota(I32, (SB, SB), 1) < lax.broadcasted_iota(I32, (SB, SB), 0),
                          1.0, 0.0).astype(BF16)

        def drop(j, seen):
            subs = [slice(r, r + SB) for r in range(0, KC, SB)]
            blk = [sc_scr[j, sl, :] for sl in subs]
            tied = [(x == thr) & tie for x in blk]
            ind = [jnp.where(x, 1.0, 0.0) for x in tied]
            rank = [_dot(below, x.astype(BF16)) for x in ind]
            for sl, x, t, i, r in zip(subs, blk, tied, ind, rank):
                sc_scr[j, sl, :] = jnp.where(t & (r + seen >= need), -jnp.inf, x)
                seen = seen + r[SB - 1:SB, :] + i[SB - 1:SB, :]
            return seen

        lax.fori_loop(0, nch, drop, jnp.zeros((1, TQ), F32))

    qm = head_halves(q_ref, ATT_HEADS)
    HD = ATT_HEAD_DIM
    heads = range(ATT_HEADS)
    ones_rows = jnp.ones((16, KC), BF16)

    def chunk_operands(j):
        start = pl.multiple_of(j * KC, KC)
        bias = jnp.where(sc_scr[j] >= thr, 0.0, NEG_BIG)
        kp = [k_ref[0, pl.ds(start, KC), pr * LANES:(pr + 1) * LANES] for pr in range(ATT_HEADS // 2)]
        vt = [jnp.concatenate([vt_ref[0, j, hh * HD:(hh + 1) * HD, :], ones_rows], axis=0) for hh in heads]
        return bias, kp, vt

    qt = q_ref[0].astype(F32).T
    in_reach = lax.broadcasted_iota(I32, kn_ref.shape[1:], 0) < nch
    kmax = jnp.max(jnp.where(in_reach, kn_ref[0], 0.0), axis=0)
    shift = []
    for hh in heads:
        qh = qt[hh * HD:(hh + 1) * HD, :]
        shift.append(jnp.sqrt(jnp.sum(qh * qh, axis=0, keepdims=True) * kmax[hh:hh + 1, 0:1]))
    acc_scr[...] = jnp.zeros(acc_scr.shape, F32)

    def p3_fast(j, carry):
        bias, kp, vt = chunk_operands(j)
        acc_old = [acc_scr[hh] for hh in heads]
        p = [jnp.exp((_dot_nt(kp[hh // 2], qm[hh]) + bias - shift[hh]).astype(BF16)) for hh in heads]
        pv = [_dot(vt[hh], p[hh]) for hh in heads]
        for hh in heads:
            acc_scr[hh] = acc_old[hh] + pv[hh]
        return carry

    lax.fori_loop(0, nch, p3_fast, 0)

    lsum = [acc_scr[hh, HD:HD + 1, :] for hh in heads]
    weak = functools.reduce(jnp.logical_or, [~(x >= 1e-25) for x in lsum])

    @pl.when(jnp.max(jnp.where(weak, 1.0, 0.0)) > 0.0)
    def _():
        m_scr[...] = jnp.full(m_scr.shape, NEG_BIG, F32)
        acc_scr[...] = jnp.zeros(acc_scr.shape, F32)

        def p3_safe(j, carry):
            bias, kp, vt = chunk_operands(j)
            m_old = [m_scr[hh:hh + 1, :] for hh in heads]
            acc_old = [acc_scr[hh] for hh in heads]
            s = [_dot_nt(kp[hh // 2], qm[hh]) + bias for hh in heads]
            m_new = [jnp.maximum(m_old[hh], jnp.max(_fold8(s[hh], jnp.maximum), axis=0, keepdims=True))
                     for hh in heads]
            p = [jnp.exp((s[hh] - m_new[hh]).astype(BF16)) for hh in heads]
            pv = [_dot(vt[hh], p[hh]) for hh in heads]
            for hh in heads:
                m_scr[hh:hh + 1, :] = m_new[hh]
                acc_scr[hh] = jnp.exp(m_old[hh] - m_new[hh]) * acc_old[hh] + pv[hh]
            return carry

        lax.fori_loop(0, nch, p3_safe, 0)

    for pr in range(ATT_HEADS // 2):
        halves = [acc_scr[hh, 0:HD, :] / acc_scr[hh, HD:HD + 1, :] for hh in (2 * pr, 2 * pr + 1)]
        o_ref[0, :, pr * LANES:(pr + 1) * LANES] = jnp.concatenate(halves, axis=0).T.astype(o_ref.dtype)


def _attention(q, qi, w, k, vt, ki, kn, *, TQ, KC):
    B, L, _ = q.shape
    top_k = min(TOPK_MAX, L // 4)
    nq = L // TQ
    blk = lambda width: pl.BlockSpec((1, TQ, width), lambda b, i: (b, i, 0))
    whole = lambda width: pl.BlockSpec((1, L, width), lambda b, i: (b, 0, 0),
                                       pipeline_mode=pl.Buffered(1))
    vt_spec = pl.BlockSpec((1, L // KC, 512, KC), lambda b, i: (b, 0, 0, 0), pipeline_mode=pl.Buffered(1))
    kn_spec = pl.BlockSpec((1, L // KC, ATT_HEADS, LANES), lambda b, i: (b, 0, 0, 0))
    return pl.pallas_call(
        functools.partial(_attn_body, TQ=TQ, KC=KC, L=L, top_k=top_k),
        grid=(B, nq),
        in_specs=[blk(512), blk(256), blk(LANES), whole(512), vt_spec, whole(LANES), kn_spec],
        out_specs=blk(512),
        out_shape=jax.ShapeDtypeStruct((B, L, 512), BF16),
        scratch_shapes=[
            pltpu.VMEM((L // KC, KC, TQ), F32),
            pltpu.VMEM((8, TQ), F32),
            pltpu.VMEM((ATT_HEADS, TQ), F32),
            pltpu.VMEM((ATT_HEADS, ATT_HEAD_DIM + 16, TQ), F32),
        ],
        compiler_params=pltpu.CompilerParams(
            dimension_semantics=("arbitrary", "arbitrary"), vmem_limit_bytes=VMEM_LIMIT),
        name="dsa_attention",
    )(q, qi, w, k, vt, ki, kn)


def _mm(a, b):
    return jnp.dot(a.astype(BF16), b.astype(BF16), preferred_element_type=F32)


def _gdn_body(q_ref, k_ref, v_ref, z_ref, g_ref, b_ref, gn_ref, o_ref, s_scr, *, nb, G):
    C = GDN_CHUNK

    @pl.when(pl.program_id(0) == 0)
    def _():
        s_scr[...] = jnp.zeros_like(s_scr)

    ri = lax.broadcasted_iota(I32, (C, C), 0)
    ci = lax.broadcasted_iota(I32, (C, C), 1)
    incl = ci <= ri
    strict = ci < ri
    eye = ci == ri
    lane = lax.broadcasted_iota(I32, (C, LANES), 1)
    chains = [(b, hd) for b in range(nb) for hd in range(GDN_HEADS)]
    units = [(gi, b, hd) for gi in range(G) for b, hd in chains]
    n = range(len(units))
    rows = [slice(gi * C, (gi + 1) * C) for gi, _, _ in units]
    cols = [slice(hd * GDN_HEAD_DIM, (hd + 1) * GDN_HEAD_DIM) for _, _, hd in units]
    q = [q_ref[b, rows[i], cols[i]] for i, (_, b, _) in enumerate(units)]
    k = [k_ref[b, rows[i], cols[i]] for i, (_, b, _) in enumerate(units)]
    v = [v_ref[b, rows[i], cols[i]] for i, (_, b, _) in enumerate(units)]
    z = [z_ref[b, rows[i], cols[i]] for i, (_, b, _) in enumerate(units)]
    g_col = [jnp.sum(jnp.where(lane == hd, g_ref[b, rows[i], :], 0.0), axis=1, keepdims=True)
             for i, (_, b, hd) in enumerate(units)]
    beta = [jnp.sum(jnp.where(lane == hd, b_ref[b, rows[i], :], 0.0), axis=1, keepdims=True)
            for i, (_, b, hd) in enumerate(units)]
    s_in = [s_scr[c] for c in range(len(chains))]
    g_row = [jnp.sum(jnp.where(eye, jnp.broadcast_to(x, (C, C)), 0.0), axis=0, keepdims=True) for x in g_col]
    gc_col = [jnp.sum(jnp.where(incl, jnp.broadcast_to(x, (C, C)), 0.0), axis=1, keepdims=True) for x in g_row]
    gc_row = [jnp.sum(jnp.where(ri <= ci, jnp.broadcast_to(x, (C, C)), 0.0), axis=0, keepdims=True) for x in g_col]
    decay = [jnp.where(incl, jnp.exp(jnp.where(incl, gc_col[i] - gc_row[i], 0.0)), 0.0) for i in n]
    eg = [jnp.exp(x) for x in gc_col]
    g_last = [x[C - 1:C, :] for x in gc_col]
    kb = [k[i] * beta[i] for i in n]
    k_bf = [x.astype(BF16) for x in k]
    a_mat = [jnp.where(strict, _dot_nt(kb[i].astype(BF16), k_bf[i]) * decay[i], 0.0) for i in n]
    qk = [jnp.where(incl, _dot_nt(q[i].astype(BF16), k_bf[i]) * decay[i], 0.0) for i in n]
    t_inv = [jnp.where(eye, 1.0, 0.0) - a for a in a_mat]
    pw = a_mat
    for _ in range(5):
        pw = [_mm(x, x) for x in pw]
        t_inv = [t_inv[i] + _mm(t_inv[i], pw[i]) for i in n]
    u = [_mm(t_inv[i], v[i] * beta[i]) for i in n]
    w = [_mm(t_inv[i], kb[i] * eg[i]) for i in n]
    q_dec = [q[i] * eg[i] for i in n]
    k_tail_t = [(k[i] * jnp.exp(g_last[i] - gc_col[i])).T for i in n]
    a_last = [jnp.exp(x) for x in g_last]
    gate = [x * jax.nn.sigmoid(x) for x in z]
    state = s_in
    outs = []
    for gi in range(G):
        ids = [gi * len(chains) + c for c in range(len(chains))]
        v_new = [u[i] - _mm(w[i], state[c]) for c, i in enumerate(ids)]
        qs = [_mm(q_dec[i], state[c]) for c, i in enumerate(ids)]
        state = [state[c] * a_last[i] + _mm(k_tail_t[i], v_new[c]) for c, i in enumerate(ids)]
        outs += [qs[c] + _mm(qk[i], v_new[c]) for c, i in enumerate(ids)]
    for c in range(len(chains)):
        s_scr[c] = state[c]
    for i, (_, b, _) in enumerate(units):
        o_ref[b, rows[i], cols[i]] = (_rms(outs[i], gn_ref[...]) * gate[i]).astype(o_ref.dtype)


def _gdn(gq, gk, gv, z, g, beta, gn, *, G):
    B, L, W = gq.shape
    R = G * GDN_CHUNK
    tok = lambda width: pl.BlockSpec((B, R, width), lambda n: (0, n, 0))
    return pl.pallas_call(
        functools.partial(_gdn_body, nb=B, G=G),
        grid=(L // R,),
        in_specs=[tok(W), tok(W), tok(W), tok(W), tok(LANES), tok(LANES),
                  pl.BlockSpec(gn.shape, lambda n: (0, 0))],
        out_specs=tok(W),
        out_shape=jax.ShapeDtypeStruct((B, L, W), BF16),
        scratch_shapes=[pltpu.VMEM((B * GDN_HEADS, GDN_HEAD_DIM, GDN_HEAD_DIM), F32)],
        compiler_params=pltpu.CompilerParams(
            dimension_semantics=("arbitrary",), vmem_limit_bytes=VMEM_LIMIT),
        name="gdn",
    )(gq, gk, gv, z, g, beta, gn)


def _out_body(x_ref, a_ref, d_ref, wa_ref, wd_ref, g_ref, o_ref):
    m = _dot(a_ref[...], wa_ref[...]) + _dot(d_ref[...], wd_ref[...])
    o_ref[...] = x_ref[...] + _rms(m, g_ref[...])


def _out_proj(x, att, gdn, wa, wd, post_g, *, tm):
    T, D = x.shape
    W = att.shape[1]
    row = lambda width: pl.BlockSpec((tm, width), lambda i: (i, 0))
    full = lambda a: pl.BlockSpec(a.shape, lambda i: (0, 0))
    return pl.pallas_call(
        _out_body,
        grid=(T // tm,),
        in_specs=[row(D), row(W), row(W), full(wa), full(wd), full(post_g)],
        out_specs=row(D),
        out_shape=jax.ShapeDtypeStruct((T, D), F32),
        compiler_params=pltpu.CompilerParams(
            dimension_semantics=("parallel",), vmem_limit_bytes=VMEM_LIMIT),
        name="out_proj",
    )(x, att, gdn, wa, wd, post_g)


def _rope_tables(L, rot_dim):
    half = rot_dim // 2
    inv_freq = 1.0 / (ROPE_THETA ** (jnp.arange(half, dtype=F32) / half))
    ang = jnp.arange(L, dtype=jnp.int32).astype(F32)[:, None] * inv_freq[None, :]
    cos, sin = jnp.cos(ang), jnp.sin(ang)
    pad = 64 - rot_dim
    cos64 = jnp.concatenate([cos, cos, jnp.ones((L, pad), F32)], axis=1)
    sin64 = jnp.concatenate([-sin, sin, jnp.zeros((L, pad), F32)], axis=1)
    return jnp.tile(cos64, (1, 2)), jnp.tile(sin64, (1, 2))


def _pad_cols(a, width):
    return jnp.pad(a, ((0, 0), (0, width - a.shape[1])))


def _layer(x, p, *, tm_ffn, tf, tm_out, TQ, KC):
    B, L, D = x.shape
    T = B * L
    row = lambda a: a.reshape(1, -1)
    x2 = _ffn(x.reshape(T, D), row(p['ffn1_pre_g']), row(p['ffn1_post_g']),
              p['ffn1_w_gate'].astype(BF16), p['ffn1_w_up'].astype(BF16), p['ffn1_w_down'].astype(BF16),
              tm=tm_ffn, tf=tf)

    w_in = p['w_in']
    sizes = (Q_LORA_RANK, 512, 512, IDX_DIM, IDX_HEADS, 512, 512, 512, 512, GDN_HEADS, GDN_HEADS)
    offs = [0]
    for s in sizes:
        offs.append(offs[-1] + s)
    col = lambda a, b: w_in[:, offs[a]:offs[b]]
    watt = jnp.concatenate([col(0, 3), _pad_cols(col(3, 4), LANES)], axis=1).astype(BF16)
    wsm = jnp.concatenate([_pad_cols(col(4, 5), LANES), _pad_cols(col(9, 10), LANES),
                           _pad_cols(col(10, 11), LANES)], axis=1).astype(BF16)
    wgdn = col(5, 8).astype(BF16)
    wz = col(8, 9).astype(BF16)
    cosa, sina = _rope_tables(L, ATT_HEAD_DIM)
    cosi, sini = _rope_tables(L, IDX_ROPE_DIM)
    q, k, vt, qi, ki, w, gq, gk, gv, z, beta, g, kn = _prep(
        x2.reshape(B, L, D), row(p['mix_pre_g']), watt, wsm, wgdn, wz, row(p['q_a_norm_g']),
        p['w_q_b'].astype(BF16), p['w_qidx_b'].astype(BF16),
        _pad_cols(row(p['kidx_ln_g']), LANES), _pad_cols(row(p['kidx_ln_b']), LANES),
        p['gdn_conv_w'], _pad_cols(row(p['gdn_a_log']), LANES), _pad_cols(row(p['gdn_dt_bias']), LANES),
        cosa, sina, cosi, sini, tm=KC)

    att = _attention(q, qi, w, k, vt, ki, kn, TQ=TQ, KC=KC)
    gdn = _gdn(gq, gk, gv, z, g, beta, row(p['gdn_out_norm_g']), G=4)
    w_out = p['w_out'].astype(BF16)
    x3 = _out_proj(x2, att.reshape(T, -1), gdn.reshape(T, -1), w_out[:512], w_out[512:],
                   row(p['mix_post_g']), tm=tm_out)
    x4 = _ffn(x3, row(p['ffn2_pre_g']), row(p['ffn2_post_g']),
              p['ffn2_w_gate'].astype(BF16), p['ffn2_w_up'].astype(BF16), p['ffn2_w_down'].astype(BF16),
              tm=tm_ffn, tf=tf)
    return x4.reshape(B, L, D)


_NAMES = ('ffn1_pre_g', 'ffn1_post_g', 'ffn1_w_gate', 'ffn1_w_up', 'ffn1_w_down', 'mix_pre_g',
          'mix_post_g', 'w_in', 'q_a_norm_g', 'w_q_b', 'w_qidx_b', 'kidx_ln_g', 'kidx_ln_b',
          'gdn_conv_w', 'gdn_a_log', 'gdn_dt_bias', 'gdn_out_norm_g', 'w_out', 'ffn2_pre_g',
          'ffn2_post_g', 'ffn2_w_gate', 'ffn2_w_up', 'ffn2_w_down')


def kernel(x, ffn1_pre_g, ffn1_post_g, ffn1_w_gate, ffn1_w_up, ffn1_w_down, mix_pre_g, mix_post_g, w_in, q_a_norm_g, w_q_b, w_qidx_b, kidx_ln_g, kidx_ln_b, gdn_conv_w, gdn_a_log, gdn_dt_bias, gdn_out_norm_g, w_out, ffn2_pre_g, ffn2_post_g, ffn2_w_gate, ffn2_w_up, ffn2_w_down):
    args = (ffn1_pre_g, ffn1_post_g, ffn1_w_gate, ffn1_w_up, ffn1_w_down, mix_pre_g, mix_post_g,
            w_in, q_a_norm_g, w_q_b, w_qidx_b, kidx_ln_g, kidx_ln_b, gdn_conv_w, gdn_a_log,
            gdn_dt_bias, gdn_out_norm_g, w_out, ffn2_pre_g, ffn2_post_g, ffn2_w_gate, ffn2_w_up,
            ffn2_w_down)
    for layer in range(ffn1_pre_g.shape[0]):
        p = {n: a[layer] for n, a in zip(_NAMES, args)}
        L = x.shape[1]
        x = _layer(x, p, tm_ffn=min(512, L), tf=ffn1_w_gate.shape[-1], tm_out=min(1024, L),
                   TQ=min(512, L), KC=min(512, L))
    return x
```

```python
import functools

import jax
import jax.numpy as jnp
from jax import lax
from jax.experimental import pallas as pl
from jax.experimental.pallas import tpu as pltpu

F32 = jnp.float32
BF16 = jnp.bfloat16
I32 = jnp.int32

LANES = 128
ATT_HEAD_DIM = 64
ATT_HEADS = 8
IDX_HEADS = 4
IDX_DIM = 64
IDX_ROPE_DIM = 32
Q_LORA_RANK = 256
TOPK_MAX = 256
GDN_HEAD_DIM = 128
GDN_HEADS = 4
GDN_CHUNK = 64
ROPE_THETA = 10000.0
NORM_EPS = 1e-6
NEG_BIG = -1e30

VMEM_LIMIT = 56 * 1024 * 1024


def _rms(x, g):
    return x * lax.rsqrt(jnp.mean(x * x, axis=-1, keepdims=True) + NORM_EPS) * g


def _dot(a, b):
    return jnp.dot(a, b, preferred_element_type=F32)


def _dot_nt(a, b):
    return lax.dot_general(a, b, (((1,), (1,)), ((), ())), preferred_element_type=F32)


def _ffn_body(x_ref, pre_ref, post_ref, wg_ref, wu_ref, wd_ref, o_ref, xn_ref, acc_ref):
    j = pl.program_id(1)

    @pl.when(j == 0)
    def _():
        xn_ref[...] = _rms(x_ref[...], pre_ref[...]).astype(BF16)
        acc_ref[...] = jnp.zeros_like(acc_ref)

    xn = xn_ref[...]
    g = _dot(xn, wg_ref[...])
    u = _dot(xn, wu_ref[...])
    h = (g * jax.nn.sigmoid(g) * u).astype(BF16)
    acc_ref[...] += _dot(h, wd_ref[...])

    @pl.when(j == pl.num_programs(1) - 1)
    def _():
        o_ref[...] = x_ref[...] + 0.5 * _rms(acc_ref[...], post_ref[...])


def _ffn(x, pre_g, post_g, wg, wu, wd, *, tm, tf):
    T, D = x.shape
    F = wg.shape[1]
    mode = dict(pipeline_mode=pl.Buffered(1)) if tf == F else {}
    return pl.pallas_call(
        _ffn_body,
        grid=(T // tm, F // tf),
        in_specs=[
            pl.BlockSpec((tm, D), lambda i, j: (i, 0)),
            pl.BlockSpec((1, D), lambda i, j: (0, 0)),
            pl.BlockSpec((1, D), lambda i, j: (0, 0)),
            pl.BlockSpec((D, tf), lambda i, j: (0, j), **mode),
            pl.BlockSpec((D, tf), lambda i, j: (0, j), **mode),
            pl.BlockSpec((tf, D), lambda i, j: (j, 0), **mode),
        ],
        out_specs=pl.BlockSpec((tm, D), lambda i, j: (i, 0)),
        out_shape=jax.ShapeDtypeStruct((T, D), F32),
        scratch_shapes=[pltpu.VMEM((tm, D), BF16), pltpu.VMEM((tm, D), F32)],
        compiler_params=pltpu.CompilerParams(
            dimension_semantics=("parallel", "arbitrary"), vmem_limit_bytes=VMEM_LIMIT),
        name="ffn",
    )(x, pre_g, post_g, wg, wu, wd)


def _rope(x, cos, sin_signed, half):
    n, W = x.shape
    lane = lax.broadcasted_iota(I32, (n, LANES), 1)
    first = (lane & 63) < half
    outs = []
    for s in range(W // LANES):
        xs = x[:, s * LANES:(s + 1) * LANES]
        partner = jnp.where(first, pltpu.roll(xs, LANES - half, 1), pltpu.roll(xs, half, 1))
        outs.append(xs * cos + partner * sin_signed)
    return outs[0] if len(outs) == 1 else jnp.concatenate(outs, axis=1)


def _prep_body(x_ref, pre_ref, watt_ref, wsm_ref, wgdn_ref, wz_ref, qan_ref, wqb_ref, wqib_ref,
               lng_ref, lnb_ref, conv_ref, alog_ref, dtb_ref, cosa_ref, sina_ref, cosi_ref, sini_ref,
               q_o, k_o, vt_o, qi_o, ki_o, w_o, gq_o, gk_o, gv_o, z_o, beta_o, g_o, kn_o,
               pbuf, *, tm):
    tile = pl.program_id(1)
    h = _rms(x_ref[0], pre_ref[...]).astype(BF16)
    cosa, sina = cosa_ref[...], sina_ref[...]
    cosi, sini = cosi_ref[...], sini_ref[...]

    pa = _dot(h, watt_ref[...])
    cq = _rms(pa[:, :Q_LORA_RANK], qan_ref[...]).astype(BF16)
    q = _rope(_dot(cq, wqb_ref[...]), cosa, sina, ATT_HEAD_DIM // 2)
    q_o[0] = (q * (ATT_HEAD_DIM ** -0.5)).astype(BF16)
    qi_o[0] = _rope(_dot(cq, wqib_ref[...]), cosi, sini, IDX_ROPE_DIM // 2).astype(BF16)
    k_bf = _rope(pa[:, 256:768], cosa, sina, ATT_HEAD_DIM // 2).astype(BF16)
    k_o[0] = k_bf
    vt_o[0, 0] = pa[:, 768:1280].T.astype(BF16)

    lane = lax.broadcasted_iota(I32, (tm, LANES), 1)
    real = lane < IDX_DIM
    lower = lane < ATT_HEAD_DIM
    k_sq = jnp.square(k_bf.astype(F32))
    head_row = lax.broadcasted_iota(I32, (ATT_HEADS, LANES), 0)
    kn = jnp.zeros((ATT_HEADS, LANES), F32)
    for hh in range(ATT_HEADS):
        slab = k_sq[:, (hh // 2) * LANES:(hh // 2 + 1) * LANES]
        norm_sq = jnp.sum(jnp.where(lower if hh % 2 == 0 else ~lower, slab, 0.0), axis=1, keepdims=True)
        kn = jnp.where(head_row == hh, jnp.max(norm_sq, axis=0, keepdims=True), kn)
    kn_o[0, 0] = kn

    ki = pa[:, 1280:1408]
    mu = jnp.sum(jnp.where(real, ki, 0.0), axis=-1, keepdims=True) * (1.0 / IDX_DIM)
    xc = jnp.where(real, ki - mu, 0.0)
    var = jnp.sum(xc * xc, axis=-1, keepdims=True) * (1.0 / IDX_DIM)
    kn = xc * lax.rsqrt(var + NORM_EPS) * lng_ref[...] + lnb_ref[...]
    kr = _rope(kn, cosi, sini, IDX_ROPE_DIM // 2)
    ki_o[0] = jnp.where(real, kr, pltpu.roll(kr, IDX_DIM, 1)).astype(BF16)

    ps = _dot(h, wsm_ref[...])
    w_o[0] = ps[:, :LANES]
    beta_o[0] = jax.nn.sigmoid(ps[:, LANES:2 * LANES])
    xa = ps[:, 2 * LANES:] + dtb_ref[...]
    softplus = jnp.maximum(xa, 0.0) + jnp.log(1.0 + jnp.exp(-jnp.abs(xa)))
    g_o[0] = -jnp.exp(alog_ref[...]) * softplus

    pg = _dot(h, wgdn_ref[...])

    @pl.when(tile == 0)
    def _():
        pbuf[0:8, :] = jnp.zeros((8, pbuf.shape[1]), F32)

    @pl.when(tile > 0)
    def _():
        pbuf[0:8, :] = pbuf[tm:tm + 8, :]

    pbuf[8:tm + 8, :] = pg
    conv = pbuf[5:5 + tm, :] * conv_ref[0:1, :]
    conv = conv + pbuf[6:6 + tm, :] * conv_ref[1:2, :]
    conv = conv + pbuf[7:7 + tm, :] * conv_ref[2:3, :]
    conv = conv + pg * conv_ref[3:4, :]
    act = conv * jax.nn.sigmoid(conv)
    W = GDN_HEADS * GDN_HEAD_DIM
    for hd in range(GDN_HEADS):
        sl = slice(hd * GDN_HEAD_DIM, (hd + 1) * GDN_HEAD_DIM)
        xq = act[:, hd * GDN_HEAD_DIM:(hd + 1) * GDN_HEAD_DIM]
        xk = act[:, W + hd * GDN_HEAD_DIM:W + (hd + 1) * GDN_HEAD_DIM]
        qn = xq * lax.rsqrt(jnp.sum(xq * xq, axis=-1, keepdims=True) + NORM_EPS)
        gq_o[0, :, sl] = qn * (GDN_HEAD_DIM ** -0.5)
        gk_o[0, :, sl] = xk * lax.rsqrt(jnp.sum(xk * xk, axis=-1, keepdims=True) + NORM_EPS)
    gv_o[0] = act[:, 2 * W:]
    z_o[0] = _dot(h, wz_ref[...])


def _prep(x, pre_g, watt, wsm, wgdn, wz, qan, wqb, wqib, lng, lnb, convw, alog, dtb,
          cosa, sina, cosi, sini, *, tm):
    B, L, D = x.shape
    n = L // tm
    full = lambda a: pl.BlockSpec(a.shape, lambda b, i: (0, 0))
    tab = pl.BlockSpec((tm, LANES), lambda b, i: (i, 0))
    tok = lambda w: pl.BlockSpec((1, tm, w), lambda b, i: (b, i, 0))
    vt_spec = pl.BlockSpec((1, 1, 512, tm), lambda b, i: (b, i, 0, 0))
    kn_spec = pl.BlockSpec((1, 1, ATT_HEADS, LANES), lambda b, i: (b, i, 0, 0))
    outs = [(512, BF16), (512, BF16), (512, BF16), (256, BF16), (LANES, BF16), (LANES, F32),
            (512, F32), (512, F32), (512, F32), (512, F32), (LANES, F32), (LANES, F32)]
    return pl.pallas_call(
        functools.partial(_prep_body, tm=tm),
        grid=(B, n),
        in_specs=[tok(D), full(pre_g), full(watt), full(wsm), full(wgdn), full(wz), full(qan),
                  full(wqb), full(wqib), full(lng), full(lnb), full(convw), full(alog), full(dtb),
                  tab, tab, tab, tab],
        out_specs=[vt_spec if idx == 2 else tok(w) for idx, (w, _) in enumerate(outs)] + [kn_spec],
        out_shape=[jax.ShapeDtypeStruct((B, n, 512, tm) if idx == 2 else (B, L, w), dt)
                   for idx, (w, dt) in enumerate(outs)]
                  + [jax.ShapeDtypeStruct((B, n, ATT_HEADS, LANES), F32)],
        scratch_shapes=[pltpu.VMEM((tm + 8, wgdn.shape[1]), F32)],
        compiler_params=pltpu.CompilerParams(
            dimension_semantics=("arbitrary", "arbitrary"), vmem_limit_bytes=VMEM_LIMIT),
        name="prep",
    )(x, pre_g, watt, wsm, wgdn, wz, qan, wqb, wqib, lng, lnb, convw, alog, dtb,
      cosa, sina, cosi, sini)


def _fold8(x, op, ways=4):
    parts = [x[r:r + 8] for r in range(0, x.shape[0], 8)]
    accs = parts[:ways]
    for idx, part in enumerate(parts[ways:]):
        accs[idx % ways] = op(accs[idx % ways], part)
    while len(accs) > 1:
        accs = [op(a, b) for a, b in zip(accs[0::2], accs[1::2])] + ([accs[-1]] if len(accs) % 2 else [])
    return accs[0]


def _attn_body(q_ref, qi_ref, w_ref, k_ref, vt_ref, ki_ref, kn_ref, o_ref,
               sc_scr, st_scr, m_scr, acc_scr, *, TQ, KC, L, top_k):
    i = pl.program_id(1)
    t0 = i * TQ
    nch = (t0 + TQ + KC - 1) // KC
    qpos = t0 + lax.broadcasted_iota(I32, (1, TQ), 1)
    lane = lax.broadcasted_iota(I32, (TQ, LANES), 1)
    lower = lane < ATT_HEAD_DIM
    zero_bf = jnp.zeros((TQ, LANES), BF16)
    rmc = lax.broadcasted_iota(I32, (KC, TQ), 0) - lax.broadcasted_iota(I32, (KC, TQ), 1)

    def head_halves(ref, n_heads):
        out = []
        for hh in range(n_heads):
            slab = ref[0, :, (hh // 2) * LANES:(hh // 2 + 1) * LANES]
            out.append(jnp.where(lower if hh % 2 == 0 else ~lower, slab, zero_bf))
        return out

    wt = w_ref[0].T
    wrow = [wt[hh:hh + 1, :] for hh in range(IDX_HEADS)]
    qim = head_halves(qi_ref, IDX_HEADS)

    def p1(j, carry):
        lo, hi = carry
        kk = ki_ref[0, pl.ds(pl.multiple_of(j * KC, KC), KC), :]
        sc = jnp.zeros((KC, TQ), F32)
        for hh in range(IDX_HEADS):
            sc = sc + wrow[hh] * jnp.maximum(_dot_nt(kk, qim[hh]), 0.0)
        masked = jnp.where(rmc <= t0 - j * KC, sc, -jnp.inf)
        sc_scr[j] = masked
        lo = jnp.minimum(lo, jnp.min(_fold8(sc, jnp.minimum), axis=0, keepdims=True))
        hi = jnp.maximum(hi, jnp.max(_fold8(masked, jnp.maximum), axis=0, keepdims=True))
        return lo, hi

    lo, top = lax.fori_loop(0, nch, p1, (jnp.full((1, TQ), jnp.inf, F32), jnp.full((1, TQ), -jnp.inf, F32)))

    krow = jnp.minimum(top_k, qpos + 1).astype(F32)

    def count(pred, ways=4):
        def body(j, accs):
            accs = list(accs)
            for r in range(KC // 8):
                hit = pred(sc_scr[j, r * 8:(r + 1) * 8, :], j * KC + r * 8)
                accs[r % ways] = accs[r % ways] + jnp.where(hit, 1.0, 0.0)
            return tuple(accs)
        accs = lax.fori_loop(0, nch, body, tuple(jnp.zeros((8, TQ), F32) for _ in range(ways)))
        return jnp.sum(functools.reduce(jnp.add, accs), axis=0, keepdims=True)

    ncausal = (qpos + 1).astype(F32)
    hi = top + (jnp.abs(top) * (2.0 ** -20) + 1e-30)
    c_ge0 = count(lambda blk, k0: blk >= 0.0)
    c_gt0 = count(lambda blk, k0: blk > 0.0)
    all_rows = ncausal == krow
    at_zero = (c_gt0 < krow) & (c_ge0 >= krow) & ~all_rows
    up = (c_ge0 >= krow) & (lo < 0.0)
    down = (c_ge0 < krow) & (hi > 0.0)
    LO, HI, CLO, CHI, DONE = range(5)
    st_scr[LO:LO + 1, :] = jnp.where(up, 0.0, lo)
    st_scr[CLO:CLO + 1, :] = jnp.where(up, c_ge0, ncausal)
    st_scr[HI:HI + 1, :] = jnp.where(down, 0.0, hi)
    st_scr[CHI:CHI + 1, :] = jnp.where(at_zero, c_gt0, jnp.where(down, c_ge0, 0.0))
    done0 = jnp.where(at_zero | all_rows, 1.0, 0.0)
    st_scr[DONE:DONE + 1, :] = done0

    def halve():
        lo, hi, done = st_scr[LO:LO + 1, :], st_scr[HI:HI + 1, :], st_scr[DONE:DONE + 1, :]
        mid = 0.5 * lo + 0.5 * hi
        stuck = (mid <= lo) | (mid >= hi)
        c = count(lambda blk, k0: blk >= mid)
        ge = c >= krow
        upd = (done == 0.0) & ~stuck
        st_scr[LO:LO + 1, :] = jnp.where(upd & ge, mid, lo)
        st_scr[CLO:CLO + 1, :] = jnp.where(upd & ge, c, st_scr[CLO:CLO + 1, :])
        st_scr[HI:HI + 1, :] = jnp.where(upd & ~ge, mid, hi)
        st_scr[CHI:CHI + 1, :] = jnp.where(upd & ~ge, c, st_scr[CHI:CHI + 1, :])
        new_done = jnp.where(stuck | (upd & (c == krow)), 1.0, done)
        st_scr[DONE:DONE + 1, :] = new_done
        return new_done

    def bisect(active):
        halve()
        return jnp.max(jnp.where(halve() == 0.0, 1.0, 0.0)).astype(I32)

    lax.while_loop(lambda a: a > 0, bisect, jnp.max(jnp.where(done0 == 0.0, 1.0, 0.0)).astype(I32))

    thr = st_scr[LO:LO + 1, :]
    tie = st_scr[CLO:CLO + 1, :] > krow
    need = krow - st_scr[CHI:CHI + 1, :]

    @pl.when(jnp.max(jnp.where(tie, 1.0, 0.0)) > 0.0)
    def _():
        SB = min(128, KC)
        below = jnp.where(lax.broadcasted_iota(I32, (SB, SB), 1) < lax.broadcasted_iota(I32, (SB, SB), 0),
                          1.0, 0.0).astype(BF16)

        def drop(j, seen):
            subs = [slice(r, r + SB) for r in range(0, KC, SB)]
            blk = [sc_scr[j, sl, :] for sl in subs]
            tied = [(x == thr) & tie for x in blk]
            ind = [jnp.where(x, 1.0, 0.0) for x in tied]
            rank = [_dot(below, x.astype(BF16)) for x in ind]
            for sl, x, t, i, r in zip(subs, blk, tied, ind, rank):
                sc_scr[j, sl, :] = jnp.where(t & (r + seen >= need), -jnp.inf, x)
                seen = seen + r[SB - 1:SB, :] + i[SB - 1:SB, :]
            return seen

        lax.fori_loop(0, nch, drop, jnp.zeros((1, TQ), F32))

    qm = head_halves(q_ref, ATT_HEADS)
    HD = ATT_HEAD_DIM
    heads = range(ATT_HEADS)
    ones_rows = jnp.ones((16, KC), BF16)

    def chunk_operands(j):
        start = pl.multiple_of(j * KC, KC)
        bias = jnp.where(sc_scr[j] >= thr, 0.0, NEG_BIG)
        kp = [k_ref[0, pl.ds(start, KC), pr * LANES:(pr + 1) * LANES] for pr in range(ATT_HEADS // 2)]
        vt = [jnp.concatenate([vt_ref[0, j, hh * HD:(hh + 1) * HD, :], ones_rows], axis=0) for hh in heads]
        return bias, kp, vt

    qt = q_ref[0].astype(F32).T
    in_reach = lax.broadcasted_iota(I32, kn_ref.shape[1:], 0) < nch
    kmax = jnp.max(jnp.where(in_reach, kn_ref[0], 0.0), axis=0)
    shift = []
    for hh in heads:
        qh = qt[hh * HD:(hh + 1) * HD, :]
        shift.append(jnp.sqrt(jnp.sum(qh * qh, axis=0, keepdims=True) * kmax[hh:hh + 1, 0:1]))
    acc_scr[...] = jnp.zeros(acc_scr.shape, F32)

    def p3_fast(j, carry):
        bias, kp, vt = chunk_operands(j)
        acc_old = [acc_scr[hh] for hh in heads]
        new = []
        for pr in range(ATT_HEADS // 2):
            pair = (2 * pr, 2 * pr + 1)
            p = {hh: jnp.exp((_dot_nt(kp[pr], qm[hh]) + bias - shift[hh]).astype(BF16)) for hh in pair}
            new += [acc_old[hh] + _dot(vt[hh], p[hh]) for hh in pair]
        for hh in heads:
            acc_scr[hh] = new[hh]
        return carry

    lax.fori_loop(0, nch, p3_fast, 0)

    lsum = [acc_scr[hh, HD:HD + 1, :] for hh in heads]
    weak = functools.reduce(jnp.logical_or, [~(x >= 1e-25) for x in lsum])

    @pl.when(jnp.max(jnp.where(weak, 1.0, 0.0)) > 0.0)
    def _():
        m_scr[...] = jnp.full(m_scr.shape, NEG_BIG, F32)
        acc_scr[...] = jnp.zeros(acc_scr.shape, F32)

        def p3_safe(j, carry):
            bias, kp, vt = chunk_operands(j)
            m_old = [m_scr[hh:hh + 1, :] for hh in heads]
            acc_old = [acc_scr[hh] for hh in heads]
            s = [_dot_nt(kp[hh // 2], qm[hh]) + bias for hh in heads]
            m_new = [jnp.maximum(m_old[hh], jnp.max(_fold8(s[hh], jnp.maximum), axis=0, keepdims=True))
                     for hh in heads]
            p = [jnp.exp((s[hh] - m_new[hh]).astype(BF16)) for hh in heads]
            pv = [_dot(vt[hh], p[hh]) for hh in heads]
            for hh in heads:
                m_scr[hh:hh + 1, :] = m_new[hh]
                acc_scr[hh] = jnp.exp(m_old[hh] - m_new[hh]) * acc_old[hh] + pv[hh]
            return carry

        lax.fori_loop(0, nch, p3_safe, 0)

    for pr in range(ATT_HEADS // 2):
        halves = [acc_scr[hh, 0:HD, :] / acc_scr[hh, HD:HD + 1, :] for hh in (2 * pr, 2 * pr + 1)]
        o_ref[0, :, pr * LANES:(pr + 1) * LANES] = jnp.concatenate(halves, axis=0).T.astype(o_ref.dtype)


def _attention(q, qi, w, k, vt, ki, kn, *, TQ, KC):
    B, L, _ = q.shape
    top_k = min(TOPK_MAX, L // 4)
    nq = L // TQ
    blk = lambda width: pl.BlockSpec((1, TQ, width), lambda b, i: (b, i, 0))
    whole = lambda width: pl.BlockSpec((1, L, width), lambda b, i: (b, 0, 0),
                                       pipeline_mode=pl.Buffered(1))
    vt_spec = pl.BlockSpec((1, L // KC, 512, KC), lambda b, i: (b, 0, 0, 0), pipeline_mode=pl.Buffered(1))
    kn_spec = pl.BlockSpec((1, L // KC, ATT_HEADS, LANES), lambda b, i: (b, 0, 0, 0))
    return pl.pallas_call(
        functools.partial(_attn_body, TQ=TQ, KC=KC, L=L, top_k=top_k),
        grid=(B, nq),
        in_specs=[blk(512), blk(256), blk(LANES), whole(512), vt_spec, whole(LANES), kn_spec],
        out_specs=blk(512),
        out_shape=jax.ShapeDtypeStruct((B, L, 512), BF16),
        scratch_shapes=[
            pltpu.VMEM((L // KC, KC, TQ), F32),
            pltpu.VMEM((8, TQ), F32),
            pltpu.VMEM((ATT_HEADS, TQ), F32),
            pltpu.VMEM((ATT_HEADS, ATT_HEAD_DIM + 16, TQ), F32),
        ],
        compiler_params=pltpu.CompilerParams(
            dimension_semantics=("arbitrary", "arbitrary"), vmem_limit_bytes=VMEM_LIMIT),
        name="dsa_attention",
    )(q, qi, w, k, vt, ki, kn)


def _mm(a, b):
    return jnp.dot(a.astype(BF16), b.astype(BF16), preferred_element_type=F32)


def _gdn_body(q_ref, k_ref, v_ref, z_ref, g_ref, b_ref, gn_ref, o_ref, s_scr, *, nb, G):
    C = GDN_CHUNK

    @pl.when(pl.program_id(0) == 0)
    def _():
        s_scr[...] = jnp.zeros_like(s_scr)

    ri = lax.broadcasted_iota(I32, (C, C), 0)
    ci = lax.broadcasted_iota(I32, (C, C), 1)
    incl = ci <= ri
    strict = ci < ri
    eye = ci == ri
    lane = lax.broadcasted_iota(I32, (C, LANES), 1)
    chains = [(b, hd) for b in range(nb) for hd in range(GDN_HEADS)]
    units = [(gi, b, hd) for gi in range(G) for b, hd in chains]
    n = range(len(units))
    rows = [slice(gi * C, (gi + 1) * C) for gi, _, _ in units]
    cols = [slice(hd * GDN_HEAD_DIM, (hd + 1) * GDN_HEAD_DIM) for _, _, hd in units]
    q = [q_ref[b, rows[i], cols[i]] for i, (_, b, _) in enumerate(units)]
    k = [k_ref[b, rows[i], cols[i]] for i, (_, b, _) in enumerate(units)]
    v = [v_ref[b, rows[i], cols[i]] for i, (_, b, _) in enumerate(units)]
    z = [z_ref[b, rows[i], cols[i]] for i, (_, b, _) in enumerate(units)]
    g_col = [jnp.sum(jnp.where(lane == hd, g_ref[b, rows[i], :], 0.0), axis=1, keepdims=True)
             for i, (_, b, hd) in enumerate(units)]
    beta = [jnp.sum(jnp.where(lane == hd, b_ref[b, rows[i], :], 0.0), axis=1, keepdims=True)
            for i, (_, b, hd) in enumerate(units)]
    s_in = [s_scr[c] for c in range(len(chains))]
    g_row = [jnp.sum(jnp.where(eye, jnp.broadcast_to(x, (C, C)), 0.0), axis=0, keepdims=True) for x in g_col]
    gc_col = [jnp.sum(jnp.where(incl, jnp.broadcast_to(x, (C, C)), 0.0), axis=1, keepdims=True) for x in g_row]
    gc_row = [jnp.sum(jnp.where(ri <= ci, jnp.broadcast_to(x, (C, C)), 0.0), axis=0, keepdims=True) for x in g_col]
    decay = [jnp.where(incl, jnp.exp(jnp.where(incl, gc_col[i] - gc_row[i], 0.0)), 0.0) for i in n]
    eg = [jnp.exp(x) for x in gc_col]
    g_last = [x[C - 1:C, :] for x in gc_col]
    kb = [k[i] * beta[i] for i in n]
    k_bf = [x.astype(BF16) for x in k]
    a_mat = [jnp.where(strict, _dot_nt(kb[i].astype(BF16), k_bf[i]) * decay[i], 0.0) for i in n]
    qk = [jnp.where(incl, _dot_nt(q[i].astype(BF16), k_bf[i]) * decay[i], 0.0) for i in n]
    t_inv = [jnp.where(eye, 1.0, 0.0) - a for a in a_mat]
    pw = a_mat
    for _ in range(5):
        pw = [_mm(x, x) for x in pw]
        t_inv = [t_inv[i] + _mm(t_inv[i], pw[i]) for i in n]
    u = [_mm(t_inv[i], v[i] * beta[i]) for i in n]
    w = [_mm(t_inv[i], kb[i] * eg[i]) for i in n]
    q_dec = [q[i] * eg[i] for i in n]
    k_tail_t = [(k[i] * jnp.exp(g_last[i] - gc_col[i])).T for i in n]
    a_last = [jnp.exp(x) for x in g_last]
    gate = [x * jax.nn.sigmoid(x) for x in z]
    state = s_in
    outs = []
    for gi in range(G):
        ids = [gi * len(chains) + c for c in range(len(chains))]
        v_new = [u[i] - _mm(w[i], state[c]) for c, i in enumerate(ids)]
        qs = [_mm(q_dec[i], state[c]) for c, i in enumerate(ids)]
        state = [state[c] * a_last[i] + _mm(k_tail_t[i], v_new[c]) for c, i in enumerate(ids)]
        outs += [qs[c] + _mm(qk[i], v_new[c]) for c, i in enumerate(ids)]
    for c in range(len(chains)):
        s_scr[c] = state[c]
    for i, (_, b, _) in enumerate(units):
        o_ref[b, rows[i], cols[i]] = (_rms(outs[i], gn_ref[...]) * gate[i]).astype(o_ref.dtype)


def _gdn(gq, gk, gv, z, g, beta, gn, *, G):
    B, L, W = gq.shape
    R = G * GDN_CHUNK
    tok = lambda width: pl.BlockSpec((B, R, width), lambda n: (0, n, 0))
    return pl.pallas_call(
        functools.partial(_gdn_body, nb=B, G=G),
        grid=(L // R,),
        in_specs=[tok(W), tok(W), tok(W), tok(W), tok(LANES), tok(LANES),
                  pl.BlockSpec(gn.shape, lambda n: (0, 0))],
        out_specs=tok(W),
        out_shape=jax.ShapeDtypeStruct((B, L, W), BF16),
        scratch_shapes=[pltpu.VMEM((B * GDN_HEADS, GDN_HEAD_DIM, GDN_HEAD_DIM), F32)],
        compiler_params=pltpu.CompilerParams(
            dimension_semantics=("arbitrary",), vmem_limit_bytes=VMEM_LIMIT),
        name="gdn",
    )(gq, gk, gv, z, g, beta, gn)


def _out_body(x_ref, a_ref, d_ref, wa_ref, wd_ref, g_ref, o_ref):
    m = _dot(a_ref[...], wa_ref[...]) + _dot(d_ref[...], wd_ref[...])
    o_ref[...] = x_ref[...] + _rms(m, g_ref[...])


def _out_proj(x, att, gdn, wa, wd, post_g, *, tm):
    T, D = x.shape
    W = att.shape[1]
    row = lambda width: pl.BlockSpec((tm, width), lambda i: (i, 0))
    full = lambda a: pl.BlockSpec(a.shape, lambda i: (0, 0))
    return pl.pallas_call(
        _out_body,
        grid=(T // tm,),
        in_specs=[row(D), row(W), row(W), full(wa), full(wd), full(post_g)],
        out_specs=row(D),
        out_shape=jax.ShapeDtypeStruct((T, D), F32),
        compiler_params=pltpu.CompilerParams(
            dimension_semantics=("parallel",), vmem_limit_bytes=VMEM_LIMIT),
        name="out_proj",
    )(x, att, gdn, wa, wd, post_g)


def _rope_tables(L, rot_dim):
    half = rot_dim // 2
    inv_freq = 1.0 / (ROPE_THETA ** (jnp.arange(half, dtype=F32) / half))
    ang = jnp.arange(L, dtype=jnp.int32).astype(F32)[:, None] * inv_freq[None, :]
    cos, sin = jnp.cos(ang), jnp.sin(ang)
    pad = 64 - rot_dim
    cos64 = jnp.concatenate([cos, cos, jnp.ones((L, pad), F32)], axis=1)
    sin64 = jnp.concatenate([-sin, sin, jnp.zeros((L, pad), F32)], axis=1)
    return jnp.tile(cos64, (1, 2)), jnp.tile(sin64, (1, 2))


def _pad_cols(a, width):
    return jnp.pad(a, ((0, 0), (0, width - a.shape[1])))


def _layer(x, p, *, tm_ffn, tf, tm_out, TQ, KC):
    B, L, D = x.shape
    T = B * L
    row = lambda a: a.reshape(1, -1)
    x2 = _ffn(x.reshape(T, D), row(p['ffn1_pre_g']), row(p['ffn1_post_g']),
              p['ffn1_w_gate'].astype(BF16), p['ffn1_w_up'].astype(BF16), p['ffn1_w_down'].astype(BF16),
              tm=tm_ffn, tf=tf)

    w_in = p['w_in']
    sizes = (Q_LORA_RANK, 512, 512, IDX_DIM, IDX_HEADS, 512, 512, 512, 512, GDN_HEADS, GDN_HEADS)
    offs = [0]
    for s in sizes:
        offs.append(offs[-1] + s)
    col = lambda a, b: w_in[:, offs[a]:offs[b]]
    watt = jnp.concatenate([col(0, 3), _pad_cols(col(3, 4), LANES)], axis=1).astype(BF16)
    wsm = jnp.concatenate([_pad_cols(col(4, 5), LANES), _pad_cols(col(9, 10), LANES),
                           _pad_cols(col(10, 11), LANES)], axis=1).astype(BF16)
    wgdn = col(5, 8).astype(BF16)
    wz = col(8, 9).astype(BF16)
    cosa, sina = _rope_tables(L, ATT_HEAD_DIM)
    cosi, sini = _rope_tables(L, IDX_ROPE_DIM)
    q, k, vt, qi, ki, w, gq, gk, gv, z, beta, g, kn = _prep(
        x2.reshape(B, L, D), row(p['mix_pre_g']), watt, wsm, wgdn, wz, row(p['q_a_norm_g']),
        p['w_q_b'].astype(BF16), p['w_qidx_b'].astype(BF16),
        _pad_cols(row(p['kidx_ln_g']), LANES), _pad_cols(row(p['kidx_ln_b']), LANES),
        p['gdn_conv_w'], _pad_cols(row(p['gdn_a_log']), LANES), _pad_cols(row(p['gdn_dt_bias']), LANES),
        cosa, sina, cosi, sini, tm=KC)

    att = _attention(q, qi, w, k, vt, ki, kn, TQ=TQ, KC=KC)
    gdn = _gdn(gq, gk, gv, z, g, beta, row(p['gdn_out_norm_g']), G=4)
    w_out = p['w_out'].astype(BF16)
    x3 = _out_proj(x2, att.reshape(T, -1), gdn.reshape(T, -1), w_out[:512], w_out[512:],
                   row(p['mix_post_g']), tm=tm_out)
    x4 = _ffn(x3, row(p['ffn2_pre_g']), row(p['ffn2_post_g']),
              p['ffn2_w_gate'].astype(BF16), p['ffn2_w_up'].astype(BF16), p['ffn2_w_down'].astype(BF16),
              tm=tm_ffn, tf=tf)
    return x4.reshape(B, L, D)


_NAMES = ('ffn1_pre_g', 'ffn1_post_g', 'ffn1_w_gate', 'ffn1_w_up', 'ffn1_w_down', 'mix_pre_g',
          'mix_post_g', 'w_in', 'q_a_norm_g', 'w_q_b', 'w_qidx_b', 'kidx_ln_g', 'kidx_ln_b',
          'gdn_conv_w', 'gdn_a_log', 'gdn_dt_bias', 'gdn_out_norm_g', 'w_out', 'ffn2_pre_g',
          'ffn2_post_g', 'ffn2_w_gate', 'ffn2_w_up', 'ffn2_w_down')


def kernel(x, ffn1_pre_g, ffn1_post_g, ffn1_w_gate, ffn1_w_up, ffn1_w_down, mix_pre_g, mix_post_g, w_in, q_a_norm_g, w_q_b, w_qidx_b, kidx_ln_g, kidx_ln_b, gdn_conv_w, gdn_a_log, gdn_dt_bias, gdn_out_norm_g, w_out, ffn2_pre_g, ffn2_post_g, ffn2_w_gate, ffn2_w_up, ffn2_w_down):
    args = (ffn1_pre_g, ffn1_post_g, ffn1_w_gate, ffn1_w_up, ffn1_w_down, mix_pre_g, mix_post_g,
            w_in, q_a_norm_g, w_q_b, w_qidx_b, kidx_ln_g, kidx_ln_b, gdn_conv_w, gdn_a_log,
            gdn_dt_bias, gdn_out_norm_g, w_out, ffn2_pre_g, ffn2_post_g, ffn2_w_gate, ffn2_w_up,
            ffn2_w_down)
    for layer in range(ffn1_pre_g.shape[0]):
        p = {n: a[layer] for n, a in zip(_NAMES, args)}
        L = x.shape[1]
        x = _layer(x, p, tm_ffn=min(512, L), tf=ffn1_w_gate.shape[-1], tm_out=min(1024, L),
                   TQ=min(512, L), KC=min(512, L))
    return x
```

```python
import functools

import jax
import jax.numpy as jnp
from jax import lax
from jax.experimental import pallas as pl
from jax.experimental.pallas import tpu as pltpu

F32 = jnp.float32
BF16 = jnp.bfloat16
I32 = jnp.int32

LANES = 128
ATT_HEAD_DIM = 64
ATT_HEADS = 8
IDX_HEADS = 4
IDX_DIM = 64
IDX_ROPE_DIM = 32
Q_LORA_RANK = 256
TOPK_MAX = 256
GDN_HEAD_DIM = 128
GDN_HEADS = 4
GDN_CHUNK = 64
ROPE_THETA = 10000.0
NORM_EPS = 1e-6
NEG_BIG = -1e30

VMEM_LIMIT = 56 * 1024 * 1024


def _rms(x, g):
    return x * lax.rsqrt(jnp.mean(x * x, axis=-1, keepdims=True) + NORM_EPS) * g


def _dot(a, b):
    return jnp.dot(a, b, preferred_element_type=F32)


def _dot_nt(a, b):
    return lax.dot_general(a, b, (((1,), (1,)), ((), ())), preferred_element_type=F32)


def _ffn_body(x_ref, pre_ref, post_ref, wg_ref, wu_ref, wd_ref, o_ref, xn_ref, acc_ref):
    j = pl.program_id(1)

    @pl.when(j == 0)
    def _():
        xn_ref[...] = _rms(x_ref[...], pre_ref[...]).astype(BF16)
        acc_ref[...] = jnp.zeros_like(acc_ref)

    xn = xn_ref[...]
    g = _dot(xn, wg_ref[...])
    u = _dot(xn, wu_ref[...])
    h = (g * jax.nn.sigmoid(g) * u).astype(BF16)
    acc_ref[...] += _dot(h, wd_ref[...])

    @pl.when(j == pl.num_programs(1) - 1)
    def _():
        o_ref[...] = x_ref[...] + 0.5 * _rms(acc_ref[...], post_ref[...])


def _ffn(x, pre_g, post_g, wg, wu, wd, *, tm, tf):
    T, D = x.shape
    F = wg.shape[1]
    mode = dict(pipeline_mode=pl.Buffered(1)) if tf == F else {}
    return pl.pallas_call(
        _ffn_body,
        grid=(T // tm, F // tf),
        in_specs=[
            pl.BlockSpec((tm, D), lambda i, j: (i, 0)),
            pl.BlockSpec((1, D), lambda i, j: (0, 0)),
            pl.BlockSpec((1, D), lambda i, j: (0, 0)),
            pl.BlockSpec((D, tf), lambda i, j: (0, j), **mode),
            pl.BlockSpec((D, tf), lambda i, j: (0, j), **mode),
            pl.BlockSpec((tf, D), lambda i, j: (j, 0), **mode),
        ],
        out_specs=pl.BlockSpec((tm, D), lambda i, j: (i, 0)),
        out_shape=jax.ShapeDtypeStruct((T, D), F32),
        scratch_shapes=[pltpu.VMEM((tm, D), BF16), pltpu.VMEM((tm, D), F32)],
        compiler_params=pltpu.CompilerParams(
            dimension_semantics=("parallel", "arbitrary"), vmem_limit_bytes=VMEM_LIMIT),
        name="ffn",
    )(x, pre_g, post_g, wg, wu, wd)


def _rope(x, cos, sin_signed, half):
    n, W = x.shape
    lane = lax.broadcasted_iota(I32, (n, LANES), 1)
    first = (lane & 63) < half
    outs = []
    for s in range(W // LANES):
        xs = x[:, s * LANES:(s + 1) * LANES]
        partner = jnp.where(first, pltpu.roll(xs, LANES - half, 1), pltpu.roll(xs, half, 1))
        outs.append(xs * cos + partner * sin_signed)
    return outs[0] if len(outs) == 1 else jnp.concatenate(outs, axis=1)


def _prep_body(x_ref, pre_ref, watt_ref, wsm_ref, wgdn_ref, wz_ref, qan_ref, wqb_ref, wqib_ref,
               lng_ref, lnb_ref, conv_ref, alog_ref, dtb_ref, cosa_ref, sina_ref, cosi_ref, sini_ref,
               q_o, k_o, vt_o, qi_o, ki_o, w_o, gq_o, gk_o, gv_o, z_o, beta_o, g_o, kn_o,
               pbuf, *, tm):
    tile = pl.program_id(1)
    h = _rms(x_ref[0], pre_ref[...]).astype(BF16)
    cosa, sina = cosa_ref[...], sina_ref[...]
    cosi, sini = cosi_ref[...], sini_ref[...]

    pa = _dot(h, watt_ref[...])
    cq = _rms(pa[:, :Q_LORA_RANK], qan_ref[...]).astype(BF16)
    q = _rope(_dot(cq, wqb_ref[...]), cosa, sina, ATT_HEAD_DIM // 2)
    q_o[0] = (q * (ATT_HEAD_DIM ** -0.5)).astype(BF16)
    qi_o[0] = _rope(_dot(cq, wqib_ref[...]), cosi, sini, IDX_ROPE_DIM // 2).astype(BF16)
    k_bf = _rope(pa[:, 256:768], cosa, sina, ATT_HEAD_DIM // 2).astype(BF16)
    k_o[0] = k_bf
    vt_o[0, 0] = pa[:, 768:1280].T.astype(BF16)

    lane = lax.broadcasted_iota(I32, (tm, LANES), 1)
    real = lane < IDX_DIM
    lower = lane < ATT_HEAD_DIM
    k_sq = jnp.square(k_bf.astype(F32))
    head_row = lax.broadcasted_iota(I32, (ATT_HEADS, LANES), 0)
    kn = jnp.zeros((ATT_HEADS, LANES), F32)
    for hh in range(ATT_HEADS):
        slab = k_sq[:, (hh // 2) * LANES:(hh // 2 + 1) * LANES]
        norm_sq = jnp.sum(jnp.where(lower if hh % 2 == 0 else ~lower, slab, 0.0), axis=1, keepdims=True)
        kn = jnp.where(head_row == hh, jnp.max(norm_sq, axis=0, keepdims=True), kn)
    kn_o[0, 0] = kn

    ki = pa[:, 1280:1408]
    mu = jnp.sum(jnp.where(real, ki, 0.0), axis=-1, keepdims=True) * (1.0 / IDX_DIM)
    xc = jnp.where(real, ki - mu, 0.0)
    var = jnp.sum(xc * xc, axis=-1, keepdims=True) * (1.0 / IDX_DIM)
    kn = xc * lax.rsqrt(var + NORM_EPS) * lng_ref[...] + lnb_ref[...]
    kr = _rope(kn, cosi, sini, IDX_ROPE_DIM // 2)
    ki_o[0] = jnp.where(real, kr, pltpu.roll(kr, IDX_DIM, 1)).astype(BF16)

    ps = _dot(h, wsm_ref[...])
    w_o[0] = ps[:, :LANES]
    beta_o[0] = jax.nn.sigmoid(ps[:, LANES:2 * LANES])
    xa = ps[:, 2 * LANES:] + dtb_ref[...]
    softplus = jnp.maximum(xa, 0.0) + jnp.log(1.0 + jnp.exp(-jnp.abs(xa)))
    g_o[0] = -jnp.exp(alog_ref[...]) * softplus

    pg = _dot(h, wgdn_ref[...])

    @pl.when(tile == 0)
    def _():
        pbuf[0:8, :] = jnp.zeros((8, pbuf.shape[1]), F32)

    @pl.when(tile > 0)
    def _():
        pbuf[0:8, :] = pbuf[tm:tm + 8, :]

    pbuf[8:tm + 8, :] = pg
    conv = pbuf[5:5 + tm, :] * conv_ref[0:1, :]
    conv = conv + pbuf[6:6 + tm, :] * conv_ref[1:2, :]
    conv = conv + pbuf[7:7 + tm, :] * conv_ref[2:3, :]
    conv = conv + pg * conv_ref[3:4, :]
    act = conv * jax.nn.sigmoid(conv)
    W = GDN_HEADS * GDN_HEAD_DIM
    for hd in range(GDN_HEADS):
        sl = slice(hd * GDN_HEAD_DIM, (hd + 1) * GDN_HEAD_DIM)
        xq = act[:, hd * GDN_HEAD_DIM:(hd + 1) * GDN_HEAD_DIM]
        xk = act[:, W + hd * GDN_HEAD_DIM:W + (hd + 1) * GDN_HEAD_DIM]
        qn = xq * lax.rsqrt(jnp.sum(xq * xq, axis=-1, keepdims=True) + NORM_EPS)
        gq_o[0, :, sl] = qn * (GDN_HEAD_DIM ** -0.5)
        gk_o[0, :, sl] = xk * lax.rsqrt(jnp.sum(xk * xk, axis=-1, keepdims=True) + NORM_EPS)
    gv_o[0] = act[:, 2 * W:]
    z_o[0] = _dot(h, wz_ref[...])


def _prep(x, pre_g, watt, wsm, wgdn, wz, qan, wqb, wqib, lng, lnb, convw, alog, dtb,
          cosa, sina, cosi, sini, *, tm):
    B, L, D = x.shape
    n = L // tm
    full = lambda a: pl.BlockSpec(a.shape, lambda b, i: (0, 0))
    tab = pl.BlockSpec((tm, LANES), lambda b, i: (i, 0))
    tok = lambda w: pl.BlockSpec((1, tm, w), lambda b, i: (b, i, 0))
    vt_spec = pl.BlockSpec((1, 1, 512, tm), lambda b, i: (b, i, 0, 0))
    kn_spec = pl.BlockSpec((1, 1, ATT_HEADS, LANES), lambda b, i: (b, i, 0, 0))
    outs = [(512, BF16), (512, BF16), (512, BF16), (256, BF16), (LANES, BF16), (LANES, F32),
            (512, F32), (512, F32), (512, F32), (512, F32), (LANES, F32), (LANES, F32)]
    return pl.pallas_call(
        functools.partial(_prep_body, tm=tm),
        grid=(B, n),
        in_specs=[tok(D), full(pre_g), full(watt), full(wsm), full(wgdn), full(wz), full(qan),
                  full(wqb), full(wqib), full(lng), full(lnb), full(convw), full(alog), full(dtb),
                  tab, tab, tab, tab],
        out_specs=[vt_spec if idx == 2 else tok(w) for idx, (w, _) in enumerate(outs)] + [kn_spec],
        out_shape=[jax.ShapeDtypeStruct((B, n, 512, tm) if idx == 2 else (B, L, w), dt)
                   for idx, (w, dt) in enumerate(outs)]
                  + [jax.ShapeDtypeStruct((B, n, ATT_HEADS, LANES), F32)],
        scratch_shapes=[pltpu.VMEM((tm + 8, wgdn.shape[1]), F32)],
        compiler_params=pltpu.CompilerParams(
            dimension_semantics=("arbitrary", "arbitrary"), vmem_limit_bytes=VMEM_LIMIT),
        name="prep",
    )(x, pre_g, watt, wsm, wgdn, wz, qan, wqb, wqib, lng, lnb, convw, alog, dtb,
      cosa, sina, cosi, sini)


def _fold8(x, op, ways=4):
    parts = [x[r:r + 8] for r in range(0, x.shape[0], 8)]
    accs = parts[:ways]
    for idx, part in enumerate(parts[ways:]):
        accs[idx % ways] = op(accs[idx % ways], part)
    while len(accs) > 1:
        accs = [op(a, b) for a, b in zip(accs[0::2], accs[1::2])] + ([accs[-1]] if len(accs) % 2 else [])
    return accs[0]


def _attn_body(q_ref, qi_ref, w_ref, k_ref, vt_ref, ki_ref, kn_ref, o_ref,
               sc_scr, st_scr, m_scr, acc_scr, *, TQ, KC, L, top_k):
    i = pl.program_id(1)
    t0 = i * TQ
    nch = (t0 + TQ + KC - 1) // KC
    qpos = t0 + lax.broadcasted_iota(I32, (1, TQ), 1)
    rmc = lax.broadcasted_iota(I32, (KC, TQ), 0) - lax.broadcasted_iota(I32, (KC, TQ), 1)
    upper = lax.broadcasted_iota(I32, (LANES, TQ), 0) >= ATT_HEAD_DIM

    def head_halves(ref, n_heads):
        xt = ref[0].astype(F32).T
        out = []
        for hh in range(n_heads):
            slab = xt[(hh // 2) * LANES:(hh // 2 + 1) * LANES, :]
            out.append(jnp.where(upper if hh % 2 else ~upper, slab, 0.0).astype(BF16))
        return out, xt

    wt = w_ref[0].T
    wrow = [wt[hh:hh + 1, :] for hh in range(IDX_HEADS)]
    qim, _ = head_halves(qi_ref, IDX_HEADS)

    def p1(j, carry):
        lo, hi = carry
        SP = min(256, KC)
        parts = []
        for r0 in range(0, KC, SP):
            kk = ki_ref[0, pl.ds(pl.multiple_of(j * KC + r0, SP), SP), :]
            part = jnp.zeros((SP, TQ), F32)
            for hh in range(IDX_HEADS):
                part = part + wrow[hh] * jnp.maximum(_dot(kk, qim[hh]), 0.0)
            parts.append(part)
        sc = jnp.concatenate(parts, axis=0)
        masked = jnp.where(rmc <= t0 - j * KC, sc, -jnp.inf)
        sc_scr[j] = masked
        lo = jnp.minimum(lo, jnp.min(_fold8(sc, jnp.minimum), axis=0, keepdims=True))
        hi = jnp.maximum(hi, jnp.max(_fold8(masked, jnp.maximum), axis=0, keepdims=True))
        return lo, hi

    lo, top = lax.fori_loop(0, nch, p1, (jnp.full((1, TQ), jnp.inf, F32), jnp.full((1, TQ), -jnp.inf, F32)))

    krow = jnp.minimum(top_k, qpos + 1).astype(F32)

    def count(pred, ways=4):
        def body(j, accs):
            accs = list(accs)
            for r in range(KC // 8):
                hit = pred(sc_scr[j, r * 8:(r + 1) * 8, :], j * KC + r * 8)
                accs[r % ways] = accs[r % ways] + jnp.where(hit, 1.0, 0.0)
            return tuple(accs)
        accs = lax.fori_loop(0, nch, body, tuple(jnp.zeros((8, TQ), F32) for _ in range(ways)))
        return jnp.sum(functools.reduce(jnp.add, accs), axis=0, keepdims=True)

    ncausal = (qpos + 1).astype(F32)
    hi = top + (jnp.abs(top) * (2.0 ** -20) + 1e-30)
    c_ge0 = count(lambda blk, k0: blk >= 0.0)
    c_gt0 = count(lambda blk, k0: blk > 0.0)
    all_rows = ncausal == krow
    at_zero = (c_gt0 < krow) & (c_ge0 >= krow) & ~all_rows
    up = (c_ge0 >= krow) & (lo < 0.0)
    down = (c_ge0 < krow) & (hi > 0.0)
    LO, HI, CLO, CHI, DONE = range(5)
    st_scr[LO:LO + 1, :] = jnp.where(up, 0.0, lo)
    st_scr[CLO:CLO + 1, :] = jnp.where(up, c_ge0, ncausal)
    st_scr[HI:HI + 1, :] = jnp.where(down, 0.0, hi)
    st_scr[CHI:CHI + 1, :] = jnp.where(at_zero, c_gt0, jnp.where(down, c_ge0, 0.0))
    done0 = jnp.where(at_zero | all_rows, 1.0, 0.0)
    st_scr[DONE:DONE + 1, :] = done0

    def halve():
        lo, hi, done = st_scr[LO:LO + 1, :], st_scr[HI:HI + 1, :], st_scr[DONE:DONE + 1, :]
        mid = 0.5 * lo + 0.5 * hi
        stuck = (mid <= lo) | (mid >= hi)
        c = count(lambda blk, k0: blk >= mid)
        ge = c >= krow
        upd = (done == 0.0) & ~stuck
        st_scr[LO:LO + 1, :] = jnp.where(upd & ge, mid, lo)
        st_scr[CLO:CLO + 1, :] = jnp.where(upd & ge, c, st_scr[CLO:CLO + 1, :])
        st_scr[HI:HI + 1, :] = jnp.where(upd & ~ge, mid, hi)
        st_scr[CHI:CHI + 1, :] = jnp.where(upd & ~ge, c, st_scr[CHI:CHI + 1, :])
        new_done = jnp.where(stuck | (upd & (c == krow)), 1.0, done)
        st_scr[DONE:DONE + 1, :] = new_done
        return new_done

    def bisect(active):
        halve()
        return jnp.max(jnp.where(halve() == 0.0, 1.0, 0.0)).astype(I32)

    lax.while_loop(lambda a: a > 0, bisect, jnp.max(jnp.where(done0 == 0.0, 1.0, 0.0)).astype(I32))

    thr = st_scr[LO:LO + 1, :]
    tie = st_scr[CLO:CLO + 1, :] > krow
    need = krow - st_scr[CHI:CHI + 1, :]

    @pl.when(jnp.max(jnp.where(tie, 1.0, 0.0)) > 0.0)
    def _():
        SB = min(128, KC)
        below = jnp.where(lax.broadcasted_iota(I32, (SB, SB), 1) < lax.broadcasted_iota(I32, (SB, SB), 0),
                          1.0, 0.0).astype(BF16)

        def drop(j, seen):
            subs = [slice(r, r + SB) for r in range(0, KC, SB)]
            blk = [sc_scr[j, sl, :] for sl in subs]
            tied = [(x == thr) & tie for x in blk]
            ind = [jnp.where(x, 1.0, 0.0) for x in tied]
            rank = [_dot(below, x.astype(BF16)) for x in ind]
            for sl, x, t, i, r in zip(subs, blk, tied, ind, rank):
                sc_scr[j, sl, :] = jnp.where(t & (r + seen >= need), -jnp.inf, x)
                seen = seen + r[SB - 1:SB, :] + i[SB - 1:SB, :]
            return seen

        lax.fori_loop(0, nch, drop, jnp.zeros((1, TQ), F32))

    qm, qt = head_halves(q_ref, ATT_HEADS)
    HD = ATT_HEAD_DIM
    heads = range(ATT_HEADS)
    ones_rows = jnp.ones((16, KC), BF16)

    def chunk_operands(j):
        start = pl.multiple_of(j * KC, KC)
        bias = jnp.where(sc_scr[j] >= thr, 0.0, NEG_BIG)
        kp = [k_ref[0, pl.ds(start, KC), pr * LANES:(pr + 1) * LANES] for pr in range(ATT_HEADS // 2)]
        vt = [jnp.concatenate([vt_ref[0, j, hh * HD:(hh + 1) * HD, :], ones_rows], axis=0) for hh in heads]
        return bias, kp, vt

    in_reach = lax.broadcasted_iota(I32, kn_ref.shape[1:], 0) < nch
    kmax = jnp.max(jnp.where(in_reach, kn_ref[0], 0.0), axis=0)
    shift = []
    for hh in heads:
        qh = qt[hh * HD:(hh + 1) * HD, :]
        shift.append(jnp.sqrt(jnp.sum(qh * qh, axis=0, keepdims=True) * kmax[hh:hh + 1, 0:1]))
    acc_scr[...] = jnp.zeros(acc_scr.shape, F32)

    def p3_fast(j, carry):
        bias, kp, vt = chunk_operands(j)
        acc_old = [acc_scr[hh] for hh in heads]
        new = []
        for pr in range(ATT_HEADS // 2):
            pair = (2 * pr, 2 * pr + 1)
            p = {hh: jnp.exp((_dot(kp[pr], qm[hh]) + bias - shift[hh]).astype(BF16)) for hh in pair}
            new += [acc_old[hh] + _dot(vt[hh], p[hh]) for hh in pair]
        for hh in heads:
            acc_scr[hh] = new[hh]
        return carry

    lax.fori_loop(0, nch, p3_fast, 0)

    lsum = [acc_scr[hh, HD:HD + 1, :] for hh in heads]
    weak = functools.reduce(jnp.logical_or, [~(x >= 1e-25) for x in lsum])

    @pl.when(jnp.max(jnp.where(weak, 1.0, 0.0)) > 0.0)
    def _():
        m_scr[...] = jnp.full(m_scr.shape, NEG_BIG, F32)
        acc_scr[...] = jnp.zeros(acc_scr.shape, F32)

        def p3_safe(j, carry):
            bias, kp, vt = chunk_operands(j)
            m_old = [m_scr[hh:hh + 1, :] for hh in heads]
            acc_old = [acc_scr[hh] for hh in heads]
            s = [_dot(kp[hh // 2], qm[hh]) + bias for hh in heads]
            m_new = [jnp.maximum(m_old[hh], jnp.max(_fold8(s[hh], jnp.maximum), axis=0, keepdims=True))
                     for hh in heads]
            p = [jnp.exp((s[hh] - m_new[hh]).astype(BF16)) for hh in heads]
            pv = [_dot(vt[hh], p[hh]) for hh in heads]
            for hh in heads:
                m_scr[hh:hh + 1, :] = m_new[hh]
                acc_scr[hh] = jnp.exp(m_old[hh] - m_new[hh]) * acc_old[hh] + pv[hh]
            return carry

        lax.fori_loop(0, nch, p3_safe, 0)

    for pr in range(ATT_HEADS // 2):
        halves = [acc_scr[hh, 0:HD, :] / acc_scr[hh, HD:HD + 1, :] for hh in (2 * pr, 2 * pr + 1)]
        o_ref[0, :, pr * LANES:(pr + 1) * LANES] = jnp.concatenate(halves, axis=0).T.astype(o_ref.dtype)


def _attention(q, qi, w, k, vt, ki, kn, *, TQ, KC):
    B, L, _ = q.shape
    top_k = min(TOPK_MAX, L // 4)
    nq = L // TQ
    blk = lambda width: pl.BlockSpec((1, TQ, width), lambda b, i: (b, i, 0))
    whole = lambda width: pl.BlockSpec((1, L, width), lambda b, i: (b, 0, 0),
                                       pipeline_mode=pl.Buffered(1))
    vt_spec = pl.BlockSpec((1, L // KC, 512, KC), lambda b, i: (b, 0, 0, 0), pipeline_mode=pl.Buffered(1))
    kn_spec = pl.BlockSpec((1, L // KC, ATT_HEADS, LANES), lambda b, i: (b, 0, 0, 0))
    return pl.pallas_call(
        functools.partial(_attn_body, TQ=TQ, KC=KC, L=L, top_k=top_k),
        grid=(B, nq),
        in_specs=[blk(512), blk(256), blk(LANES), whole(512), vt_spec, whole(LANES), kn_spec],
        out_specs=blk(512),
        out_shape=jax.ShapeDtypeStruct((B, L, 512), BF16),
        scratch_shapes=[
            pltpu.VMEM((L // KC, KC, TQ), F32),
            pltpu.VMEM((8, TQ), F32),
            pltpu.VMEM((ATT_HEADS, TQ), F32),
            pltpu.VMEM((ATT_HEADS, ATT_HEAD_DIM + 16, TQ), F32),
        ],
        compiler_params=pltpu.CompilerParams(
            dimension_semantics=("arbitrary", "arbitrary"), vmem_limit_bytes=VMEM_LIMIT),
        name="dsa_attention",
    )(q, qi, w, k, vt, ki, kn)


def _mm(a, b):
    return jnp.dot(a.astype(BF16), b.astype(BF16), preferred_element_type=F32)


def _gdn_body(q_ref, k_ref, v_ref, z_ref, g_ref, b_ref, gn_ref, o_ref, s_scr, *, nb, G):
    C = GDN_CHUNK

    @pl.when(pl.program_id(0) == 0)
    def _():
        s_scr[...] = jnp.zeros_like(s_scr)

    ri = lax.broadcasted_iota(I32, (C, C), 0)
    ci = lax.broadcasted_iota(I32, (C, C), 1)
    incl = ci <= ri
    strict = ci < ri
    eye = ci == ri
    lane = lax.broadcasted_iota(I32, (C, LANES), 1)
    chains = [(b, hd) for b in range(nb) for hd in range(GDN_HEADS)]
    units = [(gi, b, hd) for gi in range(G) for b, hd in chains]
    n = range(len(units))
    rows = [slice(gi * C, (gi + 1) * C) for gi, _, _ in units]
    cols = [slice(hd * GDN_HEAD_DIM, (hd + 1) * GDN_HEAD_DIM) for _, _, hd in units]
    q = [q_ref[b, rows[i], cols[i]] for i, (_, b, _) in enumerate(units)]
    k = [k_ref[b, rows[i], cols[i]] for i, (_, b, _) in enumerate(units)]
    v = [v_ref[b, rows[i], cols[i]] for i, (_, b, _) in enumerate(units)]
    z = [z_ref[b, rows[i], cols[i]] for i, (_, b, _) in enumerate(units)]
    g_col = [jnp.sum(jnp.where(lane == hd, g_ref[b, rows[i], :], 0.0), axis=1, keepdims=True)
             for i, (_, b, hd) in enumerate(units)]
    beta = [jnp.sum(jnp.where(lane == hd, b_ref[b, rows[i], :], 0.0), axis=1, keepdims=True)
            for i, (_, b, hd) in enumerate(units)]
    s_in = [s_scr[c] for c in range(len(chains))]
    g_row = [jnp.sum(jnp.where(eye, jnp.broadcast_to(x, (C, C)), 0.0), axis=0, keepdims=True) for x in g_col]
    gc_col = [jnp.sum(jnp.where(incl, jnp.broadcast_to(x, (C, C)), 0.0), axis=1, keepdims=True) for x in g_row]
    gc_row = [jnp.sum(jnp.where(ri <= ci, jnp.broadcast_to(x, (C, C)), 0.0), axis=0, keepdims=True) for x in g_col]
    decay = [jnp.where(incl, jnp.exp(jnp.where(incl, gc_col[i] - gc_row[i], 0.0)), 0.0) for i in n]
    eg = [jnp.exp(x) for x in gc_col]
    g_last = [x[C - 1:C, :] for x in gc_col]
    kb = [k[i] * beta[i] for i in n]
    k_bf = [x.astype(BF16) for x in k]
    a_mat = [jnp.where(strict, _dot_nt(kb[i].astype(BF16), k_bf[i]) * decay[i], 0.0) for i in n]
    qk = [jnp.where(incl, _dot_nt(q[i].astype(BF16), k_bf[i]) * decay[i], 0.0) for i in n]
    t_inv = [jnp.where(eye, 1.0, 0.0) - a for a in a_mat]
    pw = a_mat
    for _ in range(5):
        pw = [_mm(x, x) for x in pw]
        t_inv = [t_inv[i] + _mm(t_inv[i], pw[i]) for i in n]
    u = [_mm(t_inv[i], v[i] * beta[i]) for i in n]
    w = [_mm(t_inv[i], kb[i] * eg[i]) for i in n]
    q_dec = [q[i] * eg[i] for i in n]
    k_tail_t = [(k[i] * jnp.exp(g_last[i] - gc_col[i])).T for i in n]
    a_last = [jnp.exp(x) for x in g_last]
    gate = [x * jax.nn.sigmoid(x) for x in z]
    state = s_in
    outs = []
    for gi in range(G):
        ids = [gi * len(chains) + c for c in range(len(chains))]
        v_new = [u[i] - _mm(w[i], state[c]) for c, i in enumerate(ids)]
        qs = [_mm(q_dec[i], state[c]) for c, i in enumerate(ids)]
        state = [state[c] * a_last[i] + _mm(k_tail_t[i], v_new[c]) for c, i in enumerate(ids)]
        outs += [qs[c] + _mm(qk[i], v_new[c]) for c, i in enumerate(ids)]
    for c in range(len(chains)):
        s_scr[c] = state[c]
    for i, (_, b, _) in enumerate(units):
        o_ref[b, rows[i], cols[i]] = (_rms(outs[i], gn_ref[...]) * gate[i]).astype(o_ref.dtype)


def _gdn(gq, gk, gv, z, g, beta, gn, *, G):
    B, L, W = gq.shape
    R = G * GDN_CHUNK
    tok = lambda width: pl.BlockSpec((B, R, width), lambda n: (0, n, 0))
    return pl.pallas_call(
        functools.partial(_gdn_body, nb=B, G=G),
        grid=(L // R,),
        in_specs=[tok(W), tok(W), tok(W), tok(W), tok(LANES), tok(LANES),
                  pl.BlockSpec(gn.shape, lambda n: (0, 0))],
        out_specs=tok(W),
        out_shape=jax.ShapeDtypeStruct((B, L, W), BF16),
        scratch_shapes=[pltpu.VMEM((B * GDN_HEADS, GDN_HEAD_DIM, GDN_HEAD_DIM), F32)],
        compiler_params=pltpu.CompilerParams(
            dimension_semantics=("arbitrary",), vmem_limit_bytes=VMEM_LIMIT),
        name="gdn",
    )(gq, gk, gv, z, g, beta, gn)


def _out_body(x_ref, a_ref, d_ref, wa_ref, wd_ref, g_ref, o_ref):
    m = _dot(a_ref[...], wa_ref[...]) + _dot(d_ref[...], wd_ref[...])
    o_ref[...] = x_ref[...] + _rms(m, g_ref[...])


def _out_proj(x, att, gdn, wa, wd, post_g, *, tm):
    T, D = x.shape
    W = att.shape[1]
    row = lambda width: pl.BlockSpec((tm, width), lambda i: (i, 0))
    full = lambda a: pl.BlockSpec(a.shape, lambda i: (0, 0))
    return pl.pallas_call(
        _out_body,
        grid=(T // tm,),
        in_specs=[row(D), row(W), row(W), full(wa), full(wd), full(post_g)],
        out_specs=row(D),
        out_shape=jax.ShapeDtypeStruct((T, D), F32),
        compiler_params=pltpu.CompilerParams(
            dimension_semantics=("parallel",), vmem_limit_bytes=VMEM_LIMIT),
        name="out_proj",
    )(x, att, gdn, wa, wd, post_g)


def _rope_tables(L, rot_dim):
    half = rot_dim // 2
    inv_freq = 1.0 / (ROPE_THETA ** (jnp.arange(half, dtype=F32) / half))
    ang = jnp.arange(L, dtype=jnp.int32).astype(F32)[:, None] * inv_freq[None, :]
    cos, sin = jnp.cos(ang), jnp.sin(ang)
    pad = 64 - rot_dim
    cos64 = jnp.concatenate([cos, cos, jnp.ones((L, pad), F32)], axis=1)
    sin64 = jnp.concatenate([-sin, sin, jnp.zeros((L, pad), F32)], axis=1)
    return jnp.tile(cos64, (1, 2)), jnp.tile(sin64, (1, 2))


def _pad_cols(a, width):
    return jnp.pad(a, ((0, 0), (0, width - a.shape[1])))


def _layer(x, p, *, tm_ffn, tf, tm_out, TQ, KC):
    B, L, D = x.shape
    T = B * L
    row = lambda a: a.reshape(1, -1)
    x2 = _ffn(x.reshape(T, D), row(p['ffn1_pre_g']), row(p['ffn1_post_g']),
              p['ffn1_w_gate'].astype(BF16), p['ffn1_w_up'].astype(BF16), p['ffn1_w_down'].astype(BF16),
              tm=tm_ffn, tf=tf)

    w_in = p['w_in']
    sizes = (Q_LORA_RANK, 512, 512, IDX_DIM, IDX_HEADS, 512, 512, 512, 512, GDN_HEADS, GDN_HEADS)
    offs = [0]
    for s in sizes:
        offs.append(offs[-1] + s)
    col = lambda a, b: w_in[:, offs[a]:offs[b]]
    watt = jnp.concatenate([col(0, 3), _pad_cols(col(3, 4), LANES)], axis=1).astype(BF16)
    wsm = jnp.concatenate([_pad_cols(col(4, 5), LANES), _pad_cols(col(9, 10), LANES),
                           _pad_cols(col(10, 11), LANES)], axis=1).astype(BF16)
    wgdn = col(5, 8).astype(BF16)
    wz = col(8, 9).astype(BF16)
    cosa, sina = _rope_tables(L, ATT_HEAD_DIM)
    cosi, sini = _rope_tables(L, IDX_ROPE_DIM)
    q, k, vt, qi, ki, w, gq, gk, gv, z, beta, g, kn = _prep(
        x2.reshape(B, L, D), row(p['mix_pre_g']), watt, wsm, wgdn, wz, row(p['q_a_norm_g']),
        p['w_q_b'].astype(BF16), p['w_qidx_b'].astype(BF16),
        _pad_cols(row(p['kidx_ln_g']), LANES), _pad_cols(row(p['kidx_ln_b']), LANES),
        p['gdn_conv_w'], _pad_cols(row(p['gdn_a_log']), LANES), _pad_cols(row(p['gdn_dt_bias']), LANES),
        cosa, sina, cosi, sini, tm=KC)

    att = _attention(q, qi, w, k, vt, ki, kn, TQ=TQ, KC=KC)
    gdn = _gdn(gq, gk, gv, z, g, beta, row(p['gdn_out_norm_g']), G=4)
    w_out = p['w_out'].astype(BF16)
    x3 = _out_proj(x2, att.reshape(T, -1), gdn.reshape(T, -1), w_out[:512], w_out[512:],
                   row(p['mix_post_g']), tm=tm_out)
    x4 = _ffn(x3, row(p['ffn2_pre_g']), row(p['ffn2_post_g']),
              p['ffn2_w_gate'].astype(BF16), p['ffn2_w_up'].astype(BF16), p['ffn2_w_down'].astype(BF16),
              tm=tm_ffn, tf=tf)
    return x4.reshape(B, L, D)


_NAMES = ('ffn1_pre_g', 'ffn1_post_g', 'ffn1_w_gate', 'ffn1_w_up', 'ffn1_w_down', 'mix_pre_g',
          'mix_post_g', 'w_in', 'q_a_norm_g', 'w_q_b', 'w_qidx_b', 'kidx_ln_g', 'kidx_ln_b',
          'gdn_conv_w', 'gdn_a_log', 'gdn_dt_bias', 'gdn_out_norm_g', 'w_out', 'ffn2_pre_g',
          'ffn2_post_g', 'ffn2_w_gate', 'ffn2_w_up', 'ffn2_w_down')


def kernel(x, ffn1_pre_g, ffn1_post_g, ffn1_w_gate, ffn1_w_up, ffn1_w_down, mix_pre_g, mix_post_g, w_in, q_a_norm_g, w_q_b, w_qidx_b, kidx_ln_g, kidx_ln_b, gdn_conv_w, gdn_a_log, gdn_dt_bias, gdn_out_norm_g, w_out, ffn2_pre_g, ffn2_post_g, ffn2_w_gate, ffn2_w_up, ffn2_w_down):
    args = (ffn1_pre_g, ffn1_post_g, ffn1_w_gate, ffn1_w_up, ffn1_w_down, mix_pre_g, mix_post_g,
            w_in, q_a_norm_g, w_q_b, w_qidx_b, kidx_ln_g, kidx_ln_b, gdn_conv_w, gdn_a_log,
            gdn_dt_bias, gdn_out_norm_g, w_out, ffn2_pre_g, ffn2_post_g, ffn2_w_gate, ffn2_w_up,
            ffn2_w_down)
    for layer in range(ffn1_pre_g.shape[0]):
        p = {n: a[layer] for n, a in zip(_NAMES, args)}
        L = x.shape[1]
        x = _layer(x, p, tm_ffn=min(512, L), tf=ffn1_w_gate.shape[-1], tm_out=min(1024, L),
                   TQ=min(512, L), KC=min(512, L))
    return x
```
